```python
import math
import jax, jax.numpy as jnp
from jax import lax
import numpy as np

D_MODEL = 1024
BATCH = 16
SEQ = 2048
DEPTH = 2

GRID_W = 64
CTX_LEN = 256
N_EVEN = (DEPTH + 1) // 2
N_ODD = DEPTH // 2
MIX_WIDTH = D_MODEL
DA_HEADS = 4
DA_DK = 64
DA_DV = 2 * DA_DK
MLA_HEADS = 4
MLA_NOPE = 128
MLA_ROPE = 64
MLA_V = 128
MLA_Q_LORA = 256
MLA_KV_LORA = 128
EV_IN = 3 * DA_HEADS * 2 * DA_DK + MLA_Q_LORA + MLA_KV_LORA + MLA_ROPE
NA_HEADS = 16
NA_DH = 64
NA_KH = 8
NA_KW = 16
D_FF = -(-8 * D_MODEL // (3 * 256)) * 256
ROPE_THETA = 10000.0
Q_BLOCK = 128
EPS = 1e-6
NEG_INF = -1e30

kernel_name = "hybrid_diffattn_mla_natten_prefix_block"


def rms_norm(x, g):
    xf = x.astype(jnp.float32)
    y = xf * lax.rsqrt(jnp.mean(xf * xf, axis=-1, keepdims=True) + EPS)
    return (y * g.astype(jnp.float32)).astype(x.dtype)


def softmax_f32(s):
    return jax.nn.softmax(s.astype(jnp.float32), axis=-1)


def adaln(cond, mod_w, mod_b):
    return jnp.split(jax.nn.silu(cond) @ mod_w + mod_b, 6, axis=-1)


def modulate(h, shift, scale):
    return h * (1 + scale) + shift


def swiglu(h, w_in, w_out):
    g, u = jnp.split(h @ w_in, 2, axis=-1)
    return (jax.nn.silu(g) * u) @ w_out


def axial_rope_tables(S, dim):
    t = jnp.arange(S)
    row = (t // GRID_W).astype(jnp.float32)
    col = (t % GRID_W).astype(jnp.float32)
    half = dim // 2
    inv = ROPE_THETA ** (-jnp.arange(0, half, 2, dtype=jnp.float32) / half)
    ar = row[:, None] * inv
    ac = col[:, None] * inv
    return (jnp.cos(ar), jnp.sin(ar), jnp.cos(ac), jnp.sin(ac))


def _rotate(x, cos, sin):
    x1, x2 = jnp.split(x, 2, axis=-1)
    return jnp.concatenate([x1 * cos - x2 * sin, x1 * sin + x2 * cos], axis=-1)


def apply_axial_rope(x, tables):
    cr, sr, cc, sc = tables
    xr, xc = jnp.split(x, 2, axis=-1)
    return jnp.concatenate([_rotate(xr, cr, sr), _rotate(xc, cc, sc)], axis=-1).astype(x.dtype)


def sweep_query_blocks(fn, *qs):
    S = qs[0].shape[-2]
    nb = S // Q_BLOCK
    blocks = tuple(jnp.moveaxis(q.reshape(q.shape[:-2] + (nb, Q_BLOCK, q.shape[-1])), -3, 0) for q in qs)
    out = lax.map(lambda a: fn(*a), blocks)
    out = jnp.moveaxis(out, 0, -3)
    return out.reshape(out.shape[:-3] + (S, out.shape[-1]))


def merge_heads(*outs):
    return jnp.concatenate([o.transpose(0, 2, 1, 3).reshape(o.shape[0], o.shape[2], -1) for o in outs], axis=-1)


def even_project(h, w_in, da_q_g, da_k_g, mla_q_a_g, mla_w_uq, mla_kv_a_g, mla_w_ukv, mla_q_g, mla_k_g, mla_kr_g):
    B, T, _ = h.shape
    n_dq = DA_HEADS * 2 * DA_DK
    idx = [n_dq, 2 * n_dq, 3 * n_dq, 3 * n_dq + MLA_Q_LORA, 3 * n_dq + MLA_Q_LORA + MLA_KV_LORA]
    dq, dk, dv, cq, ckv, kr = jnp.split(h @ w_in, idx, axis=-1)
    dq = rms_norm(dq.reshape(B, T, DA_HEADS, 2, DA_DK), da_q_g).transpose(0, 2, 3, 1, 4)
    dk = rms_norm(dk.reshape(B, T, DA_HEADS, 2, DA_DK), da_k_g).transpose(0, 2, 3, 1, 4)
    dv = dv.reshape(B, T, DA_HEADS, DA_DV).transpose(0, 2, 1, 3)
    q = (rms_norm(cq, mla_q_a_g) @ mla_w_uq).reshape(B, T, MLA_HEADS, MLA_NOPE + MLA_ROPE)
    q = rms_norm(q, mla_q_g).transpose(0, 2, 1, 3)
    qn, qr = q[..., :MLA_NOPE], q[..., MLA_NOPE:]
    kv = (rms_norm(ckv, mla_kv_a_g) @ mla_w_ukv).reshape(B, T, MLA_HEADS, MLA_NOPE + MLA_V)
    kn = rms_norm(kv[..., :MLA_NOPE], mla_k_g).transpose(0, 2, 1, 3)
    mv = kv[..., MLA_NOPE:].transpose(0, 2, 1, 3)
    kr = rms_norm(kr, mla_kr_g)
    return dq, dk, dv, qn, qr, kn, kr, mv


def diff_attend(q, k, v, lam, lam_init, out_g):
    s = jnp.einsum('bhiqd,bhikd->bhiqk', q, k) * (DA_DK ** -0.5)
    p = softmax_f32(s)
    a = p[:, :, 0] - lam * p[:, :, 1]
    o = jnp.einsum('bhqk,bhkd->bhqd', a.astype(v.dtype), v)
    return rms_norm(o, out_g) * (1.0 - lam_init)


def mla_attend(qn, qr, kn, kr, v):
    s = (jnp.einsum('bhqd,bhkd->bhqk', qn, kn) + jnp.einsum('bhqd,bkd->bhqk', qr, kr)) * ((MLA_NOPE + MLA_ROPE) ** -0.5)
    p = softmax_f32(s)
    return jnp.einsum('bhqk,bhkd->bhqd', p.astype(v.dtype), v)


def even_mixer(hx, hc, need_ctx, w_in, da_q_g, da_k_g, lam, lam_init, da_out_g,
               mla_q_a_g, mla_w_uq, mla_kv_a_g, mla_w_ukv, mla_q_g, mla_k_g, mla_kr_g):
    proj = lambda h: even_project(h, w_in, da_q_g, da_k_g, mla_q_a_g, mla_w_uq, mla_kv_a_g, mla_w_ukv,
                                  mla_q_g, mla_k_g, mla_kr_g)
    dq_x, dk_x, dv_x, qn_x, qr_x, kn_x, kr_x, mv_x = proj(hx)
    dq_c, dk_c, dv_c, qn_c, qr_c, kn_c, kr_c, mv_c = proj(hc)
    S = hx.shape[1]
    tab_da = axial_rope_tables(S, DA_DK)
    tab_mla = axial_rope_tables(S, MLA_ROPE)
    dq_x = apply_axial_rope(dq_x, tab_da)
    dk_x = apply_axial_rope(dk_x, tab_da)
    qr_x = apply_axial_rope(qr_x, tab_mla)
    kr_x = apply_axial_rope(kr_x, tab_mla)
    dk_all = jnp.concatenate([dk_c, dk_x], axis=-2)
    dv_all = jnp.concatenate([dv_c, dv_x], axis=-2)
    kn_all = jnp.concatenate([kn_c, kn_x], axis=-2)
    kr_all = jnp.concatenate([kr_c, kr_x], axis=-2)
    mv_all = jnp.concatenate([mv_c, mv_x], axis=-2)
    da_x = sweep_query_blocks(lambda q: diff_attend(q, dk_all, dv_all, lam, lam_init, da_out_g), dq_x)
    mla_x = sweep_query_blocks(lambda a, b: mla_attend(a, b, kn_all, kr_all, mv_all), qn_x, qr_x)
    y_x = merge_heads(da_x, mla_x)
    y_c = None
    if need_ctx:
        da_c = diff_attend(dq_c, dk_c, dv_c, lam, lam_init, da_out_g)
        mla_c = mla_attend(qn_c, qr_c, kn_c, kr_c, mv_c)
        y_c = merge_heads(da_c, mla_c)
    return y_x, y_c


def odd_project(h, w_in, q_g, k_g):
    B, T, _ = h.shape
    q, k, v = jnp.split((h @ w_in).reshape(B, T, 3 * NA_HEADS, NA_DH), 3, axis=2)
    q = rms_norm(q, q_g).transpose(0, 2, 1, 3)
    k = rms_norm(k, k_g).transpose(0, 2, 1, 3)
    return q, k, v.transpose(0, 2, 1, 3)


def dense_attend(q, k, v):
    p = softmax_f32(jnp.einsum('bhqd,bhkd->bhqk', q, k) * (NA_DH ** -0.5))
    return jnp.einsum('bhqk,bhkd->bhqd', p.astype(v.dtype), v)


def na_latent(q, k, v, kc, vc, rpb):
    B, H, S, d = q.shape
    rows = S // GRID_W
    kh = min(NA_KH, rows)
    kg = k.reshape(B, H, rows, GRID_W, d)
    vg = v.reshape(B, H, rows, GRID_W, d)
    qg = q.reshape(B, H, rows, GRID_W, d)
    cols = jnp.arange(GRID_W)
    col_start = jnp.clip(cols - NA_KW // 2, 0, GRID_W - NA_KW)
    col_valid = (cols[None, :] >= col_start[:, None]) & (cols[None, :] < col_start[:, None] + NA_KW)
    dc_idx = jnp.clip(cols[None, :] - cols[:, None], -(NA_KW - 1), NA_KW - 1) + NA_KW - 1
    rpb_c = rpb[:, :, dc_idx]
    scale = NA_DH ** -0.5

    def row_step(args):
        r, q_row = args
        rs = jnp.clip(r - kh // 2, 0, rows - kh)
        kb = lax.dynamic_slice_in_dim(kg, rs, kh, axis=2)
        vb = lax.dynamic_slice_in_dim(vg, rs, kh, axis=2)
        dr_idx = rs + jnp.arange(kh) - r + NA_KH - 1
        bias = jnp.take(rpb_c, dr_idx, axis=1).transpose(0, 2, 1, 3)
        s_loc = jnp.einsum('bhqd,bhiwd->bhqiw', q_row, kb).astype(jnp.float32) * scale + bias
        s_loc = jnp.where(col_valid[:, None, :], s_loc, NEG_INF).reshape(B, H, GRID_W, kh * GRID_W)
        s_ctx = jnp.einsum('bhqd,bhkd->bhqk', q_row, kc).astype(jnp.float32) * scale
        p = softmax_f32(jnp.concatenate([s_loc, s_ctx], axis=-1))
        p_loc = p[..., :kh * GRID_W].reshape(B, H, GRID_W, kh, GRID_W).astype(v.dtype)
        p_ctx = p[..., kh * GRID_W:].astype(v.dtype)
        return jnp.einsum('bhqiw,bhiwd->bhqd', p_loc, vb) + jnp.einsum('bhqk,bhkd->bhqd', p_ctx, vc)

    out = lax.map(row_step, (jnp.arange(rows), jnp.moveaxis(qg, 2, 0)))
    return jnp.moveaxis(out, 0, 2).reshape(B, H, S, d)


def odd_mixer(hx, hc, need_ctx, w_in, q_g, k_g, rpb):
    q_x, k_x, v_x = odd_project(hx, w_in, q_g, k_g)
    q_c, k_c, v_c = odd_project(hc, w_in, q_g, k_g)
    y_x = merge_heads(na_latent(q_x, k_x, v_x, k_c, v_c, rpb))
    y_c = merge_heads(dense_attend(q_c, k_c, v_c)) if need_ctx else None
    return y_x, y_c


def setup_inputs(seed: int = 0) -> dict:
    key = jax.random.key(seed)
    ks = iter(jax.random.split(key, 40))
    nrm = lambda shape, s: jax.random.normal(next(ks), shape, jnp.float32) * s
    gain = lambda shape: 1.0 + 0.1 * jax.random.normal(next(ks), shape, jnp.float32)
    D = D_MODEL
    return {
        "x": nrm((BATCH, SEQ, D), 1.0),
        "c": nrm((BATCH, D), 1.0),
        "ctx": nrm((BATCH, CTX_LEN, D), 1.0),
        "c_ctx": nrm((D,), 1.0),
        "mod_w": nrm((DEPTH, D, 6 * D), 0.5 * D ** -0.5),
        "mod_b": nrm((DEPTH, 6 * D), 0.02),
        "norm_mix_g": gain((DEPTH, D)),
        "norm_ffn_g": gain((DEPTH, D)),
        "w_out": nrm((DEPTH, MIX_WIDTH, D), MIX_WIDTH ** -0.5),
        "ffn_w_in": nrm((DEPTH, D, 2 * D_FF), D ** -0.5),
        "ffn_w_out": nrm((DEPTH, D_FF, D), D_FF ** -0.5),
        "ev_w_in": nrm((N_EVEN, D, EV_IN), D ** -0.5),
        "da_q_g": gain((N_EVEN, DA_DK)),
        "da_k_g": gain((N_EVEN, DA_DK)),
        "da_lq1": nrm((N_EVEN, DA_DK), 0.1),
        "da_lk1": nrm((N_EVEN, DA_DK), 0.1),
        "da_lq2": nrm((N_EVEN, DA_DK), 0.1),
        "da_lk2": nrm((N_EVEN, DA_DK), 0.1),
        "da_out_g": gain((N_EVEN, DA_DV)),
        "mla_q_a_g": gain((N_EVEN, MLA_Q_LORA)),
        "mla_w_uq": nrm((N_EVEN, MLA_Q_LORA, MLA_HEADS * (MLA_NOPE + MLA_ROPE)), MLA_Q_LORA ** -0.5),
        "mla_kv_a_g": gain((N_EVEN, MLA_KV_LORA)),
        "mla_w_ukv": nrm((N_EVEN, MLA_KV_LORA, MLA_HEADS * (MLA_NOPE + MLA_V)), MLA_KV_LORA ** -0.5),
        "mla_q_g": gain((N_EVEN, MLA_NOPE + MLA_ROPE)),
        "mla_k_g": gain((N_EVEN, MLA_NOPE)),
        "mla_kr_g": gain((N_EVEN, MLA_ROPE)),
        "od_w_in": nrm((N_ODD, D, 3 * NA_HEADS * NA_DH), D ** -0.5),
        "na_q_g": gain((N_ODD, NA_DH)),
        "na_k_g": gain((N_ODD, NA_DH)),
        "na_rpb": nrm((N_ODD, NA_HEADS, 2 * NA_KH - 1, 2 * NA_KW - 1), 0.5),
    }


def reference(x, c, ctx, c_ctx, mod_w, mod_b, norm_mix_g, norm_ffn_g, w_out, ffn_w_in, ffn_w_out,
              ev_w_in, da_q_g, da_k_g, da_lq1, da_lk1, da_lq2, da_lk2, da_out_g,
              mla_q_a_g, mla_w_uq, mla_kv_a_g, mla_w_ukv, mla_q_g, mla_k_g, mla_kr_g,
              od_w_in, na_q_g, na_k_g, na_rpb):
    for l in range(DEPTH):
        need_ctx = l < DEPTH - 1
        sh_a, sc_a, g_a, sh_f, sc_f, g_f = [m[:, None, :] for m in adaln(c, mod_w[l], mod_b[l])]
        csh_a, csc_a, cg_a, csh_f, csc_f, cg_f = adaln(c_ctx, mod_w[l], mod_b[l])
        hx = modulate(rms_norm(x, norm_mix_g[l]), sh_a, sc_a)
        hc = modulate(rms_norm(ctx, norm_mix_g[l]), csh_a, csc_a)
        e = l // 2
        if l % 2 == 0:
            lam_init = 0.8 - 0.6 * math.exp(-0.3 * l)
            lam = (jnp.exp(jnp.sum(da_lq1[e] * da_lk1[e]).astype(jnp.float32))
                   - jnp.exp(jnp.sum(da_lq2[e] * da_lk2[e]).astype(jnp.float32)) + lam_init)
            y_x, y_c = even_mixer(hx, hc, need_ctx, ev_w_in[e], da_q_g[e], da_k_g[e], lam, lam_init, da_out_g[e],
                                  mla_q_a_g[e], mla_w_uq[e], mla_kv_a_g[e], mla_w_ukv[e],
                                  mla_q_g[e], mla_k_g[e], mla_kr_g[e])
        else:
            y_x, y_c = odd_mixer(hx, hc, need_ctx, od_w_in[e], na_q_g[e], na_k_g[e], na_rpb[e])
        x = x + g_a * (y_x @ w_out[l])
        x = x + g_f * swiglu(modulate(rms_norm(x, norm_ffn_g[l]), sh_f, sc_f), ffn_w_in[l], ffn_w_out[l])
        if need_ctx:
            ctx = ctx + cg_a * (y_c @ w_out[l])
            ctx = ctx + cg_f * swiglu(modulate(rms_norm(ctx, norm_ffn_g[l]), csh_f, csc_f), ffn_w_in[l], ffn_w_out[l])
    return x
```

```python
import functools
import math

import jax
import jax.numpy as jnp
import numpy as np
from jax import lax
from jax.experimental import pallas as pl
from jax.experimental.pallas import tpu as pltpu

D_MODEL = 1024
DEPTH = 2
GRID_W = 64
DA_HEADS = 4
DA_DK = 64
DA_DV = 2 * DA_DK
MLA_HEADS = 4
MLA_NOPE = 128
MLA_ROPE = 64
MLA_V = 128
MLA_Q_LORA = 256
MLA_KV_LORA = 128
NA_HEADS = 16
NA_DH = 64
NA_KH = 8
NA_KW = 16
D_FF = -(-8 * D_MODEL // (3 * 256)) * 256
ROPE_THETA = 10000.0
EPS = 1e-6
NEG_INF = -1e30

LANES = 128
MXU_DIM = 256
VMEM_LIMIT = 56 * 1024 * 1024

BF16 = jnp.bfloat16
F32 = jnp.float32


def _dot(a, b):
    return jnp.dot(a, b, preferred_element_type=F32)


def _dot_nt(a, b):
    return lax.dot_general(a, b, (((1,), (1,)), ((), ())), preferred_element_type=F32)


def _group_mean(x2, gmat):
    return _dot(x2.astype(BF16), gmat)


def _group_mean_tiled(x2, gmat):
    w = gmat.shape[0]
    n = x2.shape[1] // w
    if n == 1:
        return _group_mean(x2, gmat)
    return jnp.concatenate([_group_mean(x2[:, i * w:(i + 1) * w], gmat) for i in range(n)], axis=1)


def _swap16(x):
    n = x.shape[1]
    lane = lax.broadcasted_iota(jnp.int32, (1, n), 1)
    up = pltpu.roll(x, n - 16, 1)
    down = pltpu.roll(x, 16, 1)
    return jnp.where((lane % 32) < 16, up, down)


def _rope(x, cos, sin_signed):
    return x * cos + _swap16(x) * sin_signed


def _modulated_norm(x, g, shift, scale):
    y = x * lax.rsqrt(jnp.mean(x * x, axis=-1, keepdims=True) + EPS)
    return (y * g) * (1.0 + scale) + shift


def _adaln_kernel(cond_ref, w_ref, b_ref, o_ref):
    c = cond_ref[...]
    a = (c * jax.nn.sigmoid(c)).astype(BF16)
    o_ref[0] = _dot(a, w_ref[0].astype(BF16)) + b_ref[0]


def _adaln(cond, mod_w, mod_b):
    rows = cond.shape[0]
    n = mod_w.shape[2]
    tn = 1536
    return pl.pallas_call(
        _adaln_kernel,
        grid=(DEPTH, n // tn),
        in_specs=[
            pl.BlockSpec((rows, D_MODEL), lambda l, j: (0, 0)),
            pl.BlockSpec((1, D_MODEL, tn), lambda l, j: (l, 0, j)),
            pl.BlockSpec((1, 1, tn), lambda l, j: (l, 0, j)),
        ],
        out_specs=pl.BlockSpec((1, rows, tn), lambda l, j: (l, 0, j)),
        out_shape=jax.ShapeDtypeStruct((DEPTH, rows, n), F32),
        compiler_params=pltpu.CompilerParams(
            dimension_semantics=("arbitrary", "arbitrary"), vmem_limit_bytes=VMEM_LIMIT),
        name="adaln",
    )(cond, mod_w, mod_b.reshape(DEPTH, 1, n))


def _proj_even_kernel(rope, x_ref, sh_ref, sc_ref, g_ref, w_ref, g64_ref, g128_ref, g128h_ref, g256_ref,
                      g192_ref, dqg_ref, dkg_ref, qag_ref, wuq_ref, kvag_ref, wukv_ref, mqg_ref, mkg_ref,
                      krg_ref, cos_ref, sin_ref,
                      dq_ref, dk_ref, dv_ref, mq_ref, mk_ref, mv_ref):
    h = _modulated_norm(x_ref[0], g_ref[...], sh_ref[0], sc_ref[0]).astype(BF16)
    p = _dot(h, w_ref[...])
    n_dq = DA_HEADS * 2 * DA_DK
    if rope:
        cos128 = cos_ref[...]
        sin128 = sin_ref[...]
        cos512 = jnp.concatenate([cos128] * 4, axis=1)
        sin512 = jnp.concatenate([sin128] * 4, axis=1)
        lane = lax.broadcasted_iota(jnp.int32, (1, LANES), 1)
        cos_half = jnp.where(lane < MLA_ROPE, cos128, 1.0)
        sin_half = jnp.where(lane < MLA_ROPE, sin128, 0.0)

    def da_qk(v, gain, scale):
        r = lax.rsqrt(_group_mean_tiled(v * v, g64_ref[...]) + EPS)
        y = v * r * (gain * scale)
        if rope:
            y = _rope(y, cos512, sin512)
        return y.astype(BF16)

    dq_ref[0] = da_qk(p[:, 0:n_dq], dqg_ref[...], DA_DK ** -0.5)
    dk_ref[0] = da_qk(p[:, n_dq:2 * n_dq], dkg_ref[...], 1.0)
    dv_ref[0] = p[:, 2 * n_dq:3 * n_dq].astype(BF16)

    c0 = 3 * n_dq
    cq = p[:, c0:c0 + MLA_Q_LORA]
    cq = cq * lax.rsqrt(_group_mean(cq * cq, g256_ref[...]) + EPS) * qag_ref[...]
    q = _dot(cq.astype(BF16), wuq_ref[...])
    q = q * lax.rsqrt(_group_mean_tiled(q * q, g192_ref[...]) + EPS) * (mqg_ref[...] * (MLA_NOPE + MLA_ROPE) ** -0.5)

    c1 = c0 + MLA_Q_LORA
    ckv = p[:, c1:c1 + MLA_KV_LORA]
    ckv = ckv * lax.rsqrt(_group_mean(ckv * ckv, g128_ref[...]) + EPS) * kvag_ref[...]
    kv = _dot(ckv.astype(BF16), wukv_ref[...])

    c2 = c1 + MLA_KV_LORA
    kr = p[:, c2:c2 + LANES]
    kr = kr * lax.rsqrt(_group_mean(kr * kr, g128h_ref[...]) + EPS) * krg_ref[...]
    if rope:
        kr = _rope(kr, cos_half, sin_half)
    kr = kr.astype(BF16)

    mq, mk, mv = [], [], []
    for hh in range(MLA_HEADS):
        b0 = hh * 2 * LANES
        qn = q[:, b0:b0 + LANES]
        qr = q[:, b0 + LANES:b0 + 2 * LANES]
        if rope:
            qr = _rope(qr, cos_half, sin_half)
        mq += [qn.astype(BF16), qr.astype(BF16)]
        kn = kv[:, b0:b0 + LANES]
        kn = kn * lax.rsqrt(_group_mean(kn * kn, g128_ref[...]) + EPS) * mkg_ref[...]
        mk += [kn.astype(BF16), kr]
        mv.append(kv[:, b0 + LANES:b0 + 2 * LANES].astype(BF16))
    mq_ref[0] = jnp.concatenate(mq, axis=1)
    mk_ref[0] = jnp.concatenate(mk, axis=1)
    mv_ref[0] = jnp.concatenate(mv, axis=1)


def _const_spec(a):
    nd = a.ndim
    return pl.BlockSpec(a.shape, lambda b, i: (0,) * nd)


def _mod_spec(a):
    if a.shape[0] == 1:
        return pl.BlockSpec((1, 1, a.shape[2]), lambda b, i: (0, 0, 0))
    return pl.BlockSpec((1, 1, a.shape[2]), lambda b, i: (b, 0, 0))


def _proj_even(x, shift, scale, gain, consts, cos, sin, rope, tm):
    bn, t, d = x.shape
    tok = lambda w: pl.BlockSpec((1, tm, w), lambda b, i: (b, i, 0))
    rope_spec = pl.BlockSpec((tm, LANES), lambda b, i: (i, 0))
    widths = (512, 512, 512, 1024, 1024, 512)
    return pl.pallas_call(
        functools.partial(_proj_even_kernel, rope),
        grid=(bn, t // tm),
        in_specs=[tok(d), _mod_spec(shift), _mod_spec(scale), _const_spec(gain)]
                 + [_const_spec(c) for c in consts] + [rope_spec, rope_spec],
        out_specs=[tok(w) for w in widths],
        out_shape=[jax.ShapeDtypeStruct((bn, t, w), BF16) for w in widths],
        compiler_params=pltpu.CompilerParams(
            dimension_semantics=("arbitrary", "arbitrary"), vmem_limit_bytes=VMEM_LIMIT),
        name="proj_even",
    )(x, shift, scale, gain, *consts, cos, sin)


def _softmax_parts(s_list):
    m = functools.reduce(jnp.maximum, [jnp.max(s, axis=-1, keepdims=True) for s in s_list])
    e_list = [jnp.exp(s - m) for s in s_list]
    l = functools.reduce(jnp.add, [jnp.sum(e, axis=-1, keepdims=True) for e in e_list])
    return e_list, l


def _diff_attn_kernel(nseg, lam_init, *refs):
    q_ref = refs[0]
    k_refs = refs[1:1 + nseg]
    v_refs = refs[1 + nseg:1 + 2 * nseg]
    lqk_ref, og_ref, o_ref = refs[1 + 2 * nseg:]
    lqk = lqk_ref[...]
    lam = (jnp.exp(jnp.sum(lqk[0:1] * lqk[1:2], axis=-1, keepdims=True))
           - jnp.exp(jnp.sum(lqk[2:3] * lqk[3:4], axis=-1, keepdims=True)) + lam_init)
    q = q_ref[0]
    lane = lax.broadcasted_iota(jnp.int32, (1, LANES), 1)
    zero = jnp.zeros_like(q)
    q1 = jnp.where(lane < DA_DK, q, zero)
    q2 = jnp.where(lane >= DA_DK, q, zero)
    e1, l1 = _softmax_parts([_dot_nt(q1, k[0]) for k in k_refs])
    e2, l2 = _softmax_parts([_dot_nt(q2, k[0]) for k in k_refs])
    r1 = 1.0 / l1
    r2 = lam / l2
    o = functools.reduce(jnp.add, [_dot((a * r1 - b * r2).astype(BF16), v[0])
                                   for a, b, v in zip(e1, e2, v_refs)])
    o = o * lax.rsqrt(jnp.mean(o * o, axis=-1, keepdims=True) + EPS) * (og_ref[...] * (1.0 - lam_init))
    o_ref[0] = o.astype(BF16)


def _diff_attn(q, ks, vs, lqk, out_g, lam_init, tq):
    bn, t, _ = q.shape
    nseg = len(ks)
    qspec = pl.BlockSpec((1, tq, LANES), lambda b, h, i: (b, i, h))
    kvspec = lambda a: pl.BlockSpec((1, a.shape[1], LANES), lambda b, h, i: (b, 0, h))
    cspec = lambda a: pl.BlockSpec(a.shape, lambda b, h, i: (0, 0))
    return pl.pallas_call(
        functools.partial(_diff_attn_kernel, nseg, lam_init),
        grid=(bn, DA_HEADS, t // tq),
        in_specs=[qspec] + [kvspec(k) for k in ks] + [kvspec(v) for v in vs] + [cspec(lqk), cspec(out_g)],
        out_specs=qspec,
        out_shape=jax.ShapeDtypeStruct((bn, t, DA_HEADS * DA_DV), BF16),
        compiler_params=pltpu.CompilerParams(
            dimension_semantics=("arbitrary",) * 3, vmem_limit_bytes=VMEM_LIMIT),
        name="diff_attn",
    )(q, *ks, *vs, lqk, out_g)


def _mla_attn_kernel(nseg, *refs):
    q_ref = refs[0]
    k_refs = refs[1:1 + nseg]
    v_refs = refs[1 + nseg:1 + 2 * nseg]
    o_ref = refs[1 + 2 * nseg]
    q = q_ref[0]
    e, l = _softmax_parts([_dot_nt(q, k[0]) for k in k_refs])
    o = functools.reduce(jnp.add, [_dot(a.astype(BF16), v[0]) for a, v in zip(e, v_refs)])
    o_ref[0] = (o * (1.0 / l)).astype(BF16)


def _mla_attn(q, ks, vs, tq):
    bn, t, _ = q.shape
    nseg = len(ks)
    hw = 2 * LANES
    qspec = pl.BlockSpec((1, tq, hw), lambda b, h, i: (b, i, h))
    kspec = lambda a: pl.BlockSpec((1, a.shape[1], hw), lambda b, h, i: (b, 0, h))
    vspec = lambda a: pl.BlockSpec((1, a.shape[1], LANES), lambda b, h, i: (b, 0, h))
    return pl.pallas_call(
        functools.partial(_mla_attn_kernel, nseg),
        grid=(bn, MLA_HEADS, t // tq),
        in_specs=[qspec] + [kspec(k) for k in ks] + [vspec(v) for v in vs],
        out_specs=pl.BlockSpec((1, tq, LANES), lambda b, h, i: (b, i, h)),
        out_shape=jax.ShapeDtypeStruct((bn, t, MLA_HEADS * MLA_V), BF16),
        compiler_params=pltpu.CompilerParams(
            dimension_semantics=("arbitrary",) * 3, vmem_limit_bytes=VMEM_LIMIT),
        name="mla_attn",
    )(q, *ks, *vs)


def _post_kernel(ny, *refs):
    x_ref = refs[0]
    y_refs = refs[1:1 + ny]
    ga_ref, shf_ref, scf_ref, gf_ref, g_ref = refs[1 + ny:6 + ny]
    wo_refs = refs[6 + ny:6 + 2 * ny]
    win_ref, wout_ref, o_ref = refs[6 + 2 * ny:]
    attn = functools.reduce(jnp.add, [_dot(y[0], w[...]) for y, w in zip(y_refs, wo_refs)])
    x1 = x_ref[0] + ga_ref[0] * attn
    h = _modulated_norm(x1, g_ref[...], shf_ref[0], scf_ref[0]).astype(BF16)
    u = _dot(h, win_ref[...])
    gate = u[:, :D_FF]
    act = (gate * jax.nn.sigmoid(gate) * u[:, D_FF:]).astype(BF16)
    o_ref[0] = x1 + gf_ref[0] * _dot(act, wout_ref[...])


def _post(x, ys, gate_a, shift_f, scale_f, gate_f, gain, w_os, w_in, w_out, tm):
    bn, t, d = x.shape
    tok = lambda w: pl.BlockSpec((1, tm, w), lambda b, i: (b, i, 0))
    wspec = lambda a: pl.BlockSpec(a.shape, lambda b, i: (0, 0), pipeline_mode=pl.Buffered(1))
    return pl.pallas_call(
        functools.partial(_post_kernel, len(ys)),
        grid=(bn, t // tm),
        in_specs=[tok(d)] + [tok(y.shape[2]) for y in ys]
                 + [_mod_spec(gate_a), _mod_spec(shift_f), _mod_spec(scale_f), _mod_spec(gate_f), _const_spec(gain)]
                 + [wspec(w) for w in w_os] + [wspec(w_in), wspec(w_out)],
        out_specs=tok(d),
        out_shape=jax.ShapeDtypeStruct((bn, t, d), F32),
        compiler_params=pltpu.CompilerParams(
            dimension_semantics=("arbitrary", "arbitrary"), vmem_limit_bytes=VMEM_LIMIT),
        name="post",
    )(x, *ys, gate_a, shift_f, scale_f, gate_f, gain, *w_os, w_in, w_out)


def _proj_odd_kernel(x_ref, sh_ref, sc_ref, g_ref, w_ref, g64_ref, qg_ref, kg_ref, q_ref, k_ref, v_ref):
    h = _modulated_norm(x_ref[0], g_ref[...], sh_ref[0], sc_ref[0]).astype(BF16)
    p = _dot(h, w_ref[...])
    w = NA_HEADS * NA_DH

    def qk(v, gain, scale):
        r = lax.rsqrt(_group_mean_tiled(v * v, g64_ref[...]) + EPS)
        return (v * r * (gain * scale)).astype(BF16)

    q_ref[0] = qk(p[:, :w], qg_ref[...], NA_DH ** -0.5)
    k_ref[0] = qk(p[:, w:2 * w], kg_ref[...], 1.0)
    v_ref[0] = p[:, 2 * w:].astype(BF16)


def _proj_odd(x, shift, scale, gain, consts, tm):
    bn, t, d = x.shape
    tok = lambda w: pl.BlockSpec((1, tm, w), lambda b, i: (b, i, 0))
    w = NA_HEADS * NA_DH
    return pl.pallas_call(
        _proj_odd_kernel,
        grid=(bn, t // tm),
        in_specs=[tok(d), _mod_spec(shift), _mod_spec(scale), _const_spec(gain)] + [_const_spec(c) for c in consts],
        out_specs=[tok(w)] * 3,
        out_shape=[jax.ShapeDtypeStruct((bn, t, w), BF16)] * 3,
        compiler_params=pltpu.CompilerParams(
            dimension_semantics=("arbitrary", "arbitrary"), vmem_limit_bytes=VMEM_LIMIT),
        name="proj_odd",
    )(x, shift, scale, gain, *consts)


def _na_kernel(rows, q_ref, k_ref, v_ref, kc_ref, vc_ref, bias_ref, o_ref):
    lane = lax.broadcasted_iota(jnp.int32, (1, LANES), 1)
    lo = lane < NA_DH
    kc = kc_ref[0]
    vc = vc_ref[0]
    band = NA_KH * GRID_W

    def row_step(r, carry):
        rs = jnp.clip(r - NA_KH // 2, 0, rows - NA_KH)
        d0 = rs - r + NA_KH - 1
        q = q_ref[0, pl.ds(pl.multiple_of(r * GRID_W, GRID_W), GRID_W), :]
        zero = jnp.zeros_like(q)
        qq = jnp.concatenate([jnp.where(lo, q, zero), jnp.where(lo, zero, q)], axis=0)
        k0 = pl.multiple_of(rs * GRID_W, GRID_W)
        kb = k_ref[0, pl.ds(k0, band), :]
        vb = v_ref[0, pl.ds(k0, band), :]
        s_loc = _dot_nt(qq, kb) + bias_ref[0, d0]
        s_ctx = _dot_nt(qq, kc)
        (e_loc, e_ctx), l = _softmax_parts([s_loc, s_ctx])
        o = (_dot(e_loc.astype(BF16), vb) + _dot(e_ctx.astype(BF16), vc)) * (1.0 / l)
        out = jnp.where(lo, o[:GRID_W], o[GRID_W:])
        o_ref[0, pl.ds(pl.multiple_of(r * GRID_W, GRID_W), GRID_W), :] = out.astype(BF16)
        return carry

    lax.fori_loop(0, rows, row_step, 0)


def _na_attn(q, k, v, kc, vc, bias):
    bn, s, w = q.shape
    rows = s // GRID_W
    pairs = w // LANES
    tspec = lambda a: pl.BlockSpec((1, a.shape[1], LANES), lambda b, p: (b, 0, p))
    return pl.pallas_call(
        functools.partial(_na_kernel, rows),
        grid=(bn, pairs),
        in_specs=[tspec(q), tspec(k), tspec(v), tspec(kc), tspec(vc),
                  pl.BlockSpec((1,) + bias.shape[1:], lambda b, p: (p, 0, 0, 0))],
        out_specs=tspec(q),
        out_shape=jax.ShapeDtypeStruct((bn, s, w), BF16),
        compiler_params=pltpu.CompilerParams(
            dimension_semantics=("arbitrary", "arbitrary"), vmem_limit_bytes=VMEM_LIMIT),
        name="na_attn",
    )(q, k, v, kc, vc, bias)


def _group_matrix(width, group, denom):
    idx = np.arange(width) // group
    return jnp.asarray((idx[:, None] == idx[None, :]).astype(np.float32) / denom, dtype=BF16)


def _rope_tables(s):
    t = np.arange(s)
    row = (t // GRID_W).astype(np.float32)
    col = (t % GRID_W).astype(np.float32)
    half = DA_DK // 2
    inv = jnp.asarray(ROPE_THETA, F32) ** (-jnp.arange(0, half, 2, dtype=F32) / half)
    ar = jnp.asarray(row)[:, None] * inv
    ac = jnp.asarray(col)[:, None] * inv
    cr, sr, cc, sc = jnp.cos(ar), jnp.sin(ar), jnp.cos(ac), jnp.sin(ac)
    cos64 = jnp.concatenate([cr, cr, cc, cc], axis=1)
    sin64 = jnp.concatenate([-sr, sr, -sc, sc], axis=1)
    return jnp.concatenate([cos64, cos64], axis=1), jnp.concatenate([sin64, sin64], axis=1)


def _na_bias_table(rpb):
    cols = np.arange(GRID_W)
    col_start = np.clip(cols - NA_KW // 2, 0, GRID_W - NA_KW)
    valid = (cols[None, :] >= col_start[:, None]) & (cols[None, :] < col_start[:, None] + NA_KW)
    dc_idx = np.clip(cols[None, :] - cols[:, None], -(NA_KW - 1), NA_KW - 1) + NA_KW - 1
    b = jnp.where(jnp.asarray(valid)[None, None], rpb[:, :, dc_idx], NEG_INF)
    dr = np.arange(NA_KH)[:, None] + np.arange(NA_KH)[None, :]
    t = b[:, dr]
    t = t.transpose(0, 1, 3, 2, 4).reshape(NA_HEADS, NA_KH, GRID_W, NA_KH * GRID_W)
    t = t.reshape(NA_HEADS // 2, 2, NA_KH, GRID_W, NA_KH * GRID_W).transpose(0, 2, 1, 3, 4)
    return t.reshape(NA_HEADS // 2, NA_KH, 2 * GRID_W, NA_KH * GRID_W)


def _tile_lanes(g, n):
    return jnp.tile(g, n).reshape(1, -1)


def kernel(x, c, ctx, c_ctx, mod_w, mod_b, norm_mix_g, norm_ffn_g, w_out, ffn_w_in, ffn_w_out, ev_w_in, da_q_g, da_k_g, da_lq1, da_lk1, da_lq2, da_lk2, da_out_g, mla_q_a_g, mla_w_uq, mla_kv_a_g, mla_w_ukv, mla_q_g, mla_k_g, mla_kr_g, od_w_in, na_q_g, na_k_g, na_rpb):
    bsz, seq, d = x.shape
    tm = 256
    tq = 256

    cond = jnp.concatenate([c, c_ctx[None], jnp.zeros((7, d), F32)], axis=0)
    mods = _adaln(cond, mod_w, mod_b)

    def mod_vectors(l, lo, hi):
        return [mods[l, lo:hi, j * d:(j + 1) * d].reshape(hi - lo, 1, d) for j in range(6)]

    g64 = _group_matrix(MXU_DIM, 64, 64.0)
    cos, sin = _rope_tables(seq)

    l = 0
    w_in = jnp.concatenate([ev_w_in[0], jnp.zeros((d, 64), F32)], axis=1).astype(BF16)
    wuq = mla_w_uq[0].reshape(MLA_Q_LORA, MLA_HEADS, MLA_NOPE + MLA_ROPE)
    wuq = jnp.pad(wuq, ((0, 0), (0, 0), (0, 64))).reshape(MLA_Q_LORA, MLA_HEADS * 256).astype(BF16)
    mqg = jnp.tile(jnp.pad(mla_q_g[0], (0, 64)), MLA_HEADS).reshape(1, -1)
    consts = [
        w_in, g64, _group_matrix(LANES, LANES, 128.0), _group_matrix(LANES, LANES, 64.0),
        _group_matrix(MXU_DIM, MXU_DIM, 256.0), _group_matrix(MXU_DIM, MXU_DIM, 192.0),
        _tile_lanes(da_q_g[0], 8), _tile_lanes(da_k_g[0], 8),
        mla_q_a_g[0].reshape(1, -1), wuq, mla_kv_a_g[0].reshape(1, -1), mla_w_ukv[0].astype(BF16),
        mqg, mla_k_g[0].reshape(1, -1), jnp.pad(mla_kr_g[0], (0, 64)).reshape(1, -1),
    ]
    lqk = jnp.stack([da_lq1[0], da_lk1[0], da_lq2[0], da_lk2[0]])
    lam_init = 0.8 - 0.6 * math.exp(-0.3 * l)
    out_g = da_out_g[0].reshape(1, -1)
    w_o = w_out[l].astype(BF16)
    w_os = [w_o[:DA_HEADS * DA_DV], w_o[DA_HEADS * DA_DV:]]
    f_in = ffn_w_in[l].astype(BF16)
    f_out = ffn_w_out[l].astype(BF16)
    gm = norm_mix_g[l].reshape(1, d)
    gf = norm_ffn_g[l].reshape(1, d)

    sh_a, sc_a, g_a, sh_f, sc_f, g_f = mod_vectors(l, 0, bsz)
    csh_a, csc_a, cg_a, csh_f, csc_f, cg_f = mod_vectors(l, bsz, bsz + 1)

    px = _proj_even(x, sh_a, sc_a, gm, consts, cos, sin, True, tm)
    pc = _proj_even(ctx, csh_a, csc_a, gm, consts, cos[:ctx.shape[1]], sin[:ctx.shape[1]], False, tm)
    dq_x, dk_x, dv_x, mq_x, mk_x, mv_x = px
    dq_c, dk_c, dv_c, mq_c, mk_c, mv_c = pc

    da_x = _diff_attn(dq_x, [dk_c, dk_x], [dv_c, dv_x], lqk, out_g, lam_init, tq)
    mla_x = _mla_attn(mq_x, [mk_c, mk_x], [mv_c, mv_x], tq)
    da_c = _diff_attn(dq_c, [dk_c], [dv_c], lqk, out_g, lam_init, tq)
    mla_c = _mla_attn(mq_c, [mk_c], [mv_c], tq)

    x = _post(x, [da_x, mla_x], g_a, sh_f, sc_f, g_f, gf, w_os, f_in, f_out, tm)
    ctx = _post(ctx, [da_c, mla_c], cg_a, csh_f, csc_f, cg_f, gf, w_os, f_in, f_out, tm)

    l = 1
    sh_a, sc_a, g_a, sh_f, sc_f, g_f = mod_vectors(l, 0, bsz)
    csh_a, csc_a = mod_vectors(l, bsz, bsz + 1)[:2]
    gm = norm_mix_g[l].reshape(1, d)
    gf = norm_ffn_g[l].reshape(1, d)
    consts = [od_w_in[0].astype(BF16), g64, _tile_lanes(na_q_g[0], NA_HEADS), _tile_lanes(na_k_g[0], NA_HEADS)]
    q_x, k_x, v_x = _proj_odd(x, sh_a, sc_a, gm, consts, tm)
    _, k_c, v_c = _proj_odd(ctx, csh_a, csc_a, gm, consts, tm)
    y = _na_attn(q_x, k_x, v_x, k_c, v_c, _na_bias_table(na_rpb[0]))
    x = _post(x, [y], g_a, sh_f, sc_f, g_f, gf, [w_out[l].astype(BF16)],
              ffn_w_in[l].astype(BF16), ffn_w_out[l].astype(BF16), tm)
    return x
```

```python
import functools
import math

import jax
import jax.numpy as jnp
import numpy as np
from jax import lax
from jax.experimental import pallas as pl
from jax.experimental.pallas import tpu as pltpu

D_MODEL = 1024
DEPTH = 2
GRID_W = 64
DA_HEADS = 4
DA_DK = 64
DA_DV = 2 * DA_DK
MLA_HEADS = 4
MLA_NOPE = 128
MLA_ROPE = 64
MLA_V = 128
MLA_Q_LORA = 256
MLA_KV_LORA = 128
NA_HEADS = 16
NA_DH = 64
NA_KH = 8
NA_KW = 16
D_FF = -(-8 * D_MODEL // (3 * 256)) * 256
ROPE_THETA = 10000.0
EPS = 1e-6
NEG_INF = -1e30
LOG2E = math.log2(math.e)
NA_ROW_UNROLL = 8

LANES = 128
MXU_DIM = 256
VMEM_LIMIT = 56 * 1024 * 1024

BF16 = jnp.bfloat16
F32 = jnp.float32


def _dot(a, b):
    return jnp.dot(a, b, preferred_element_type=F32)


def _dot_nt(a, b):
    return lax.dot_general(a, b, (((1,), (1,)), ((), ())), preferred_element_type=F32)


def _group_rsqrt(x, gmat, size):
    x2 = (x * x).astype(BF16)
    w = gmat.shape[0]
    n = x2.shape[1] // w
    sums = [_dot(x2[:, i * w:(i + 1) * w], gmat) for i in range(n)]
    ss = sums[0] if n == 1 else jnp.concatenate(sums, axis=1)
    return lax.rsqrt(ss * (1.0 / size) + EPS)


def _swap16(x):
    n = x.shape[1]
    lane = lax.broadcasted_iota(jnp.int32, (1, n), 1)
    up = pltpu.roll(x, n - 16, 1)
    down = pltpu.roll(x, 16, 1)
    return jnp.where((lane % 32) < 16, up, down)


def _rope(x, cos, sin_signed):
    return x * cos + _swap16(x) * sin_signed


def _modulated_norm(x, g, shift, scale):
    y = x * lax.rsqrt(jnp.mean(x * x, axis=-1, keepdims=True) + EPS)
    return (y * g) * (1.0 + scale) + shift


def _adaln_kernel(cond_ref, w_ref, b_ref, o_ref):
    c = cond_ref[...]
    a = (c * jax.nn.sigmoid(c)).astype(BF16)
    o_ref[0] = _dot(a, w_ref[0].astype(BF16)) + b_ref[0]


def _adaln(cond, mod_w, mod_b):
    rows = cond.shape[0]
    n = mod_w.shape[2]
    tn = 1536
    return pl.pallas_call(
        _adaln_kernel,
        grid=(DEPTH, n // tn),
        in_specs=[
            pl.BlockSpec((rows, D_MODEL), lambda l, j: (0, 0)),
            pl.BlockSpec((1, D_MODEL, tn), lambda l, j: (l, 0, j)),
            pl.BlockSpec((1, 1, tn), lambda l, j: (l, 0, j)),
        ],
        out_specs=pl.BlockSpec((1, rows, tn), lambda l, j: (l, 0, j)),
        out_shape=jax.ShapeDtypeStruct((DEPTH, rows, n), F32),
        compiler_params=pltpu.CompilerParams(
            dimension_semantics=("arbitrary", "arbitrary"), vmem_limit_bytes=VMEM_LIMIT),
        name="adaln",
    )(cond, mod_w, mod_b.reshape(DEPTH, 1, n))


def _proj_even_kernel(rope, x_ref, sh_ref, sc_ref, g_ref, w_ref, g64_ref, g128_ref, g256_ref,
                      dqg_ref, dkg_ref, qag_ref, wuq_ref, kvag_ref, wukv_ref, mqg_ref, mkg_ref,
                      krg_ref, cos_ref, sin_ref,
                      dq_ref, dk_ref, dv_ref, mq_ref, mk_ref, mv_ref):
    h = _modulated_norm(x_ref[0], g_ref[...], sh_ref[0], sc_ref[0]).astype(BF16)
    p = _dot(h, w_ref[...])
    n_dq = DA_HEADS * 2 * DA_DK
    if rope:
        cos128 = cos_ref[...]
        sin128 = sin_ref[...]
        cos512 = jnp.concatenate([cos128] * 4, axis=1)
        sin512 = jnp.concatenate([sin128] * 4, axis=1)
        lane = lax.broadcasted_iota(jnp.int32, (1, LANES), 1)
        cos_half = jnp.where(lane < MLA_ROPE, cos128, 1.0)
        sin_half = jnp.where(lane < MLA_ROPE, sin128, 0.0)

    def da_qk(v, gain, scale):
        y = v * _group_rsqrt(v, g64_ref[...], DA_DK) * (gain * scale)
        if rope:
            y = _rope(y, cos512, sin512)
        return y.astype(BF16)

    dq_ref[0] = da_qk(p[:, 0:n_dq], dqg_ref[...], DA_DK ** -0.5 * LOG2E)
    dk_ref[0] = da_qk(p[:, n_dq:2 * n_dq], dkg_ref[...], 1.0)
    dv_ref[0] = p[:, 2 * n_dq:3 * n_dq].astype(BF16)

    c0 = 3 * n_dq
    cq = p[:, c0:c0 + MLA_Q_LORA]
    cq = cq * _group_rsqrt(cq, g256_ref[...], MLA_Q_LORA) * qag_ref[...]
    q = _dot(cq.astype(BF16), wuq_ref[...])
    q = q * _group_rsqrt(q, g256_ref[...], MLA_NOPE + MLA_ROPE) * (
        mqg_ref[...] * ((MLA_NOPE + MLA_ROPE) ** -0.5 * LOG2E))

    c1 = c0 + MLA_Q_LORA
    ckv = p[:, c1:c1 + MLA_KV_LORA]
    ckv = ckv * _group_rsqrt(ckv, g128_ref[...], MLA_KV_LORA) * kvag_ref[...]
    kv = _dot(ckv.astype(BF16), wukv_ref[...])

    c2 = c1 + MLA_KV_LORA
    kr = p[:, c2:c2 + LANES]
    kr = kr * _group_rsqrt(kr, g128_ref[...], MLA_ROPE) * krg_ref[...]
    if rope:
        kr = _rope(kr, cos_half, sin_half)
    kr = kr.astype(BF16)

    mq, mk, mv = [], [], []
    for hh in range(MLA_HEADS):
        b0 = hh * 2 * LANES
        qn = q[:, b0:b0 + LANES]
        qr = q[:, b0 + LANES:b0 + 2 * LANES]
        if rope:
            qr = _rope(qr, cos_half, sin_half)
        mq += [qn.astype(BF16), qr.astype(BF16)]
        kn = kv[:, b0:b0 + LANES]
        kn = kn * _group_rsqrt(kn, g128_ref[...], MLA_NOPE) * mkg_ref[...]
        mk += [kn.astype(BF16), kr]
        mv.append(kv[:, b0 + LANES:b0 + 2 * LANES].astype(BF16))
    mq_ref[0] = jnp.concatenate(mq, axis=1)
    mk_ref[0] = jnp.concatenate(mk, axis=1)
    mv_ref[0] = jnp.concatenate(mv, axis=1)


def _const_spec(a):
    nd = a.ndim
    return pl.BlockSpec(a.shape, lambda b, i: (0,) * nd)


def _mod_spec(a):
    if a.shape[0] == 1:
        return pl.BlockSpec((1, 1, a.shape[2]), lambda b, i: (0, 0, 0))
    return pl.BlockSpec((1, 1, a.shape[2]), lambda b, i: (b, 0, 0))


def _proj_even(x, shift, scale, gain, consts, cos, sin, rope, tm):
    bn, t, d = x.shape
    tok = lambda w: pl.BlockSpec((1, tm, w), lambda b, i: (b, i, 0))
    rope_spec = pl.BlockSpec((tm, LANES), lambda b, i: (i, 0))
    widths = (512, 512, 512, 1024, 1024, 512)
    return pl.pallas_call(
        functools.partial(_proj_even_kernel, rope),
        grid=(bn, t // tm),
        in_specs=[tok(d), _mod_spec(shift), _mod_spec(scale), _const_spec(gain)]
                 + [_const_spec(c) for c in consts] + [rope_spec, rope_spec],
        out_specs=[tok(w) for w in widths],
        out_shape=[jax.ShapeDtypeStruct((bn, t, w), BF16) for w in widths],
        compiler_params=pltpu.CompilerParams(
            dimension_semantics=("arbitrary", "arbitrary"), vmem_limit_bytes=VMEM_LIMIT),
        name="proj_even",
    )(x, shift, scale, gain, *consts, cos, sin)


def _softmax_parts(s_list):
    m = functools.reduce(jnp.maximum, [jnp.max(s, axis=-1, keepdims=True) for s in s_list])
    e_list = [jnp.exp2(s - m) for s in s_list]
    l = functools.reduce(jnp.add, [jnp.sum(e, axis=-1, keepdims=True) for e in e_list])
    return e_list, l


def _diff_attn_kernel(nseg, nsub, lam_init, *refs):
    q_ref = refs[0]
    k_refs = refs[1:1 + nseg]
    v_refs = refs[1 + nseg:1 + 2 * nseg]
    lqk_ref, og_ref, o_ref = refs[1 + 2 * nseg:]
    lqk = lqk_ref[...]
    lam = (jnp.exp(jnp.sum(lqk[0:1] * lqk[1:2], axis=-1, keepdims=True))
           - jnp.exp(jnp.sum(lqk[2:3] * lqk[3:4], axis=-1, keepdims=True)) + lam_init)
    lane = lax.broadcasted_iota(jnp.int32, (1, LANES), 1)
    rows = q_ref.shape[1] // nsub
    for t in range(nsub):
        sl = slice(t * rows, (t + 1) * rows)
        q = q_ref[0, sl, :]
        zero = jnp.zeros_like(q)
        q1 = jnp.where(lane < DA_DK, q, zero)
        q2 = jnp.where(lane >= DA_DK, q, zero)
        e1, l1 = _softmax_parts([_dot_nt(q1, k[0]) for k in k_refs])
        e2, l2 = _softmax_parts([_dot_nt(q2, k[0]) for k in k_refs])
        r1 = 1.0 / l1
        r2 = lam / l2
        o = functools.reduce(jnp.add, [_dot((a * r1 - b * r2).astype(BF16), v[0])
                                       for a, b, v in zip(e1, e2, v_refs)])
        o = o * lax.rsqrt(jnp.mean(o * o, axis=-1, keepdims=True) + EPS) * (og_ref[...] * (1.0 - lam_init))
        o_ref[0, sl, :] = o.astype(BF16)


def _diff_attn(q, ks, vs, lqk, out_g, lam_init, tq, nsub):
    bn, t, _ = q.shape
    nseg = len(ks)
    tq = min(tq, t)
    qspec = pl.BlockSpec((1, tq, LANES), lambda b, h, i: (b, i, h))
    kvspec = lambda a: pl.BlockSpec((1, a.shape[1], LANES), lambda b, h, i: (b, 0, h))
    cspec = lambda a: pl.BlockSpec(a.shape, lambda b, h, i: (0, 0))
    return pl.pallas_call(
        functools.partial(_diff_attn_kernel, nseg, nsub, lam_init),
        grid=(bn, DA_HEADS, t // tq),
        in_specs=[qspec] + [kvspec(k) for k in ks] + [kvspec(v) for v in vs] + [cspec(lqk), cspec(out_g)],
        out_specs=qspec,
        out_shape=jax.ShapeDtypeStruct((bn, t, DA_HEADS * DA_DV), BF16),
        compiler_params=pltpu.CompilerParams(
            dimension_semantics=("arbitrary",) * 3, vmem_limit_bytes=VMEM_LIMIT),
        name="diff_attn",
    )(q, *ks, *vs, lqk, out_g)


def _mla_attn_kernel(nseg, nsub, *refs):
    q_ref = refs[0]
    k_refs = refs[1:1 + nseg]
    v_refs = refs[1 + nseg:1 + 2 * nseg]
    o_ref = refs[1 + 2 * nseg]
    rows = q_ref.shape[1] // nsub
    for t in range(nsub):
        sl = slice(t * rows, (t + 1) * rows)
        q = q_ref[0, sl, :]
        e, l = _softmax_parts([_dot_nt(q, k[0]) for k in k_refs])
        o = functools.reduce(jnp.add, [_dot(a.astype(BF16), v[0]) for a, v in zip(e, v_refs)])
        o_ref[0, sl, :] = (o * (1.0 / l)).astype(BF16)


def _mla_attn(q, ks, vs, tq, nsub):
    bn, t, _ = q.shape
    nseg = len(ks)
    tq = min(tq, t)
    hw = 2 * LANES
    qspec = pl.BlockSpec((1, tq, hw), lambda b, h, i: (b, i, h))
    kspec = lambda a: pl.BlockSpec((1, a.shape[1], hw), lambda b, h, i: (b, 0, h))
    vspec = lambda a: pl.BlockSpec((1, a.shape[1], LANES), lambda b, h, i: (b, 0, h))
    return pl.pallas_call(
        functools.partial(_mla_attn_kernel, nseg, nsub),
        grid=(bn, MLA_HEADS, t // tq),
        in_specs=[qspec] + [kspec(k) for k in ks] + [vspec(v) for v in vs],
        out_specs=pl.BlockSpec((1, tq, LANES), lambda b, h, i: (b, i, h)),
        out_shape=jax.ShapeDtypeStruct((bn, t, MLA_HEADS * MLA_V), BF16),
        compiler_params=pltpu.CompilerParams(
            dimension_semantics=("arbitrary",) * 3, vmem_limit_bytes=VMEM_LIMIT),
        name="mla_attn",
    )(q, *ks, *vs)


def _post_kernel(ny, *refs):
    x_ref = refs[0]
    y_refs = refs[1:1 + ny]
    ga_ref, shf_ref, scf_ref, gf_ref, g_ref = refs[1 + ny:6 + ny]
    wo_refs = refs[6 + ny:6 + 2 * ny]
    win_ref, wout_ref, o_ref = refs[6 + 2 * ny:]
    attn = functools.reduce(jnp.add, [_dot(y[0], w[...]) for y, w in zip(y_refs, wo_refs)])
    x1 = x_ref[0] + ga_ref[0] * attn
    h = _modulated_norm(x1, g_ref[...], shf_ref[0], scf_ref[0]).astype(BF16)
    u = _dot(h, win_ref[...])
    gate = u[:, :D_FF]
    act = (gate * jax.nn.sigmoid(gate) * u[:, D_FF:]).astype(BF16)
    o_ref[0] = x1 + gf_ref[0] * _dot(act, wout_ref[...])


def _post(x, ys, gate_a, shift_f, scale_f, gate_f, gain, w_os, w_in, w_out, tm):
    bn, t, d = x.shape
    tok = lambda w: pl.BlockSpec((1, tm, w), lambda b, i: (b, i, 0))
    wspec = lambda a: pl.BlockSpec(a.shape, lambda b, i: (0, 0), pipeline_mode=pl.Buffered(1))
    return pl.pallas_call(
        functools.partial(_post_kernel, len(ys)),
        grid=(bn, t // tm),
        in_specs=[tok(d)] + [tok(y.shape[2]) for y in ys]
                 + [_mod_spec(gate_a), _mod_spec(shift_f), _mod_spec(scale_f), _mod_spec(gate_f), _const_spec(gain)]
                 + [wspec(w) for w in w_os] + [wspec(w_in), wspec(w_out)],
        out_specs=tok(d),
        out_shape=jax.ShapeDtypeStruct((bn, t, d), F32),
        compiler_params=pltpu.CompilerParams(
            dimension_semantics=("arbitrary", "arbitrary"), vmem_limit_bytes=VMEM_LIMIT),
        name="post",
    )(x, *ys, gate_a, shift_f, scale_f, gate_f, gain, *w_os, w_in, w_out)


def _proj_odd_kernel(x_ref, sh_ref, sc_ref, g_ref, w_ref, g64_ref, qg_ref, kg_ref, q_ref, k_ref, v_ref):
    h = _modulated_norm(x_ref[0], g_ref[...], sh_ref[0], sc_ref[0]).astype(BF16)
    p = _dot(h, w_ref[...])
    w = NA_HEADS * NA_DH

    def qk(v, gain, scale):
        return (v * _group_rsqrt(v, g64_ref[...], NA_DH) * (gain * scale)).astype(BF16)

    q_ref[0] = qk(p[:, :w], qg_ref[...], NA_DH ** -0.5 * LOG2E)
    k_ref[0] = qk(p[:, w:2 * w], kg_ref[...], 1.0)
    v_ref[0] = p[:, 2 * w:].astype(BF16)


def _proj_odd(x, shift, scale, gain, consts, tm):
    bn, t, d = x.shape
    tok = lambda w: pl.BlockSpec((1, tm, w), lambda b, i: (b, i, 0))
    w = NA_HEADS * NA_DH
    return pl.pallas_call(
        _proj_odd_kernel,
        grid=(bn, t // tm),
        in_specs=[tok(d), _mod_spec(shift), _mod_spec(scale), _const_spec(gain)] + [_const_spec(c) for c in consts],
        out_specs=[tok(w)] * 3,
        out_shape=[jax.ShapeDtypeStruct((bn, t, w), BF16)] * 3,
        compiler_params=pltpu.CompilerParams(
            dimension_semantics=("arbitrary", "arbitrary"), vmem_limit_bytes=VMEM_LIMIT),
        name="proj_odd",
    )(x, shift, scale, gain, *consts)


def _na_kernel(rows, q_ref, k_ref, v_ref, kc_ref, vc_ref, bias_ref, o_ref):
    lane = lax.broadcasted_iota(jnp.int32, (1, LANES), 1)
    lo = lane < NA_DH
    kc = kc_ref[0]
    vc = vc_ref[0]
    band = NA_KH * GRID_W

    def row_step(r, carry):
        rs = jnp.clip(r - NA_KH // 2, 0, rows - NA_KH)
        d0 = rs - r + NA_KH - 1
        q = q_ref[0, pl.ds(pl.multiple_of(r * GRID_W, GRID_W), GRID_W), :]
        zero = jnp.zeros_like(q)
        qq = jnp.concatenate([jnp.where(lo, q, zero), jnp.where(lo, zero, q)], axis=0)
        k0 = pl.multiple_of(rs * GRID_W, GRID_W)
        kb = k_ref[0, pl.ds(k0, band), :]
        vb = v_ref[0, pl.ds(k0, band), :]
        s_loc = _dot_nt(qq, kb) + bias_ref[0, d0]
        s_ctx = _dot_nt(qq, kc)
        (e_loc, e_ctx), l = _softmax_parts([s_loc, s_ctx])
        o = (_dot(e_loc.astype(BF16), vb) + _dot(e_ctx.astype(BF16), vc)) * (1.0 / l)
        out = jnp.where(lo, o[:GRID_W], o[GRID_W:])
        o_ref[0, pl.ds(pl.multiple_of(r * GRID_W, GRID_W), GRID_W), :] = out.astype(BF16)
        return carry

    lax.fori_loop(0, rows, row_step, 0, unroll=NA_ROW_UNROLL)


def _na_attn(q, k, v, kc, vc, bias):
    bn, s, w = q.shape
    rows = s // GRID_W
    pairs = w // LANES
    tspec = lambda a: pl.BlockSpec((1, a.shape[1], LANES), lambda p, b: (b, 0, p))
    return pl.pallas_call(
        functools.partial(_na_kernel, rows),
        grid=(pairs, bn),
        in_specs=[tspec(q), tspec(k), tspec(v), tspec(kc), tspec(vc),
                  pl.BlockSpec((1,) + bias.shape[1:], lambda p, b: (p, 0, 0, 0))],
        out_specs=tspec(q),
        out_shape=jax.ShapeDtypeStruct((bn, s, w), BF16),
        compiler_params=pltpu.CompilerParams(
            dimension_semantics=("arbitrary", "arbitrary"), vmem_limit_bytes=VMEM_LIMIT),
        name="na_attn",
    )(q, k, v, kc, vc, bias)


def _group_matrix(width, group):
    idx = np.arange(width) // group
    return jnp.asarray((idx[:, None] == idx[None, :]).astype(np.float32), dtype=BF16)


def _rope_tables(s):
    t = np.arange(s)
    row = (t // GRID_W).astype(np.float32)
    col = (t % GRID_W).astype(np.float32)
    half = DA_DK // 2
    inv = jnp.asarray(ROPE_THETA, F32) ** (-jnp.arange(0, half, 2, dtype=F32) / half)
    ar = jnp.asarray(row)[:, None] * inv
    ac = jnp.asarray(col)[:, None] * inv
    cr, sr, cc, sc = jnp.cos(ar), jnp.sin(ar), jnp.cos(ac), jnp.sin(ac)
    cos64 = jnp.concatenate([cr, cr, cc, cc], axis=1)
    sin64 = jnp.concatenate([-sr, sr, -sc, sc], axis=1)
    return jnp.concatenate([cos64, cos64], axis=1), jnp.concatenate([sin64, sin64], axis=1)


def _na_bias_table(rpb):
    cols = np.arange(GRID_W)
    col_start = np.clip(cols - NA_KW // 2, 0, GRID_W - NA_KW)
    valid = (cols[None, :] >= col_start[:, None]) & (cols[None, :] < col_start[:, None] + NA_KW)
    dc_idx = np.clip(cols[None, :] - cols[:, None], -(NA_KW - 1), NA_KW - 1) + NA_KW - 1
    b = jnp.where(jnp.asarray(valid)[None, None], rpb[:, :, dc_idx] * LOG2E, NEG_INF)
    dr = np.arange(NA_KH)[:, None] + np.arange(NA_KH)[None, :]
    t = b[:, dr]
    t = t.transpose(0, 1, 3, 2, 4).reshape(NA_HEADS, NA_KH, GRID_W, NA_KH * GRID_W)
    t = t.reshape(NA_HEADS // 2, 2, NA_KH, GRID_W, NA_KH * GRID_W).transpose(0, 2, 1, 3, 4)
    return t.reshape(NA_HEADS // 2, NA_KH, 2 * GRID_W, NA_KH * GRID_W)


def _tile_lanes(g, n):
    return jnp.tile(g, n).reshape(1, -1)


def kernel(x, c, ctx, c_ctx, mod_w, mod_b, norm_mix_g, norm_ffn_g, w_out, ffn_w_in, ffn_w_out, ev_w_in, da_q_g, da_k_g, da_lq1, da_lk1, da_lq2, da_lk2, da_out_g, mla_q_a_g, mla_w_uq, mla_kv_a_g, mla_w_ukv, mla_q_g, mla_k_g, mla_kr_g, od_w_in, na_q_g, na_k_g, na_rpb):
    bsz, seq, d = x.shape
    tm = 256
    tq, nsub = 512, 2

    cond = jnp.concatenate([c, c_ctx[None], jnp.zeros((7, d), F32)], axis=0)
    mods = _adaln(cond, mod_w, mod_b)

    def mod_vectors(l, lo, hi):
        return [mods[l, lo:hi, j * d:(j + 1) * d].reshape(hi - lo, 1, d) for j in range(6)]

    g64 = _group_matrix(MXU_DIM, 64)
    cos, sin = _rope_tables(seq)

    l = 0
    w_in = jnp.concatenate([ev_w_in[0], jnp.zeros((d, 64), F32)], axis=1).astype(BF16)
    wuq = mla_w_uq[0].reshape(MLA_Q_LORA, MLA_HEADS, MLA_NOPE + MLA_ROPE)
    wuq = jnp.pad(wuq, ((0, 0), (0, 0), (0, 64))).reshape(MLA_Q_LORA, MLA_HEADS * 256).astype(BF16)
    mqg = jnp.tile(jnp.pad(mla_q_g[0], (0, 64)), MLA_HEADS).reshape(1, -1)
    consts = [
        w_in, g64, _group_matrix(LANES, LANES), _group_matrix(MXU_DIM, MXU_DIM),
        _tile_lanes(da_q_g[0], 8), _tile_lanes(da_k_g[0], 8),
        mla_q_a_g[0].reshape(1, -1), wuq, mla_kv_a_g[0].reshape(1, -1), mla_w_ukv[0].astype(BF16),
        mqg, mla_k_g[0].reshape(1, -1), jnp.pad(mla_kr_g[0], (0, 64)).reshape(1, -1),
    ]
    lqk = jnp.stack([da_lq1[0], da_lk1[0], da_lq2[0], da_lk2[0]])
    lam_init = 0.8 - 0.6 * math.exp(-0.3 * l)
    out_g = da_out_g[0].reshape(1, -1)
    w_o = w_out[l].astype(BF16)
    w_os = [w_o[:DA_HEADS * DA_DV], w_o[DA_HEADS * DA_DV:]]
    f_in = ffn_w_in[l].astype(BF16)
    f_out = ffn_w_out[l].astype(BF16)
    gm = norm_mix_g[l].reshape(1, d)
    gf = norm_ffn_g[l].reshape(1, d)

    sh_a, sc_a, g_a, sh_f, sc_f, g_f = mod_vectors(l, 0, bsz)
    csh_a, csc_a, cg_a, csh_f, csc_f, cg_f = mod_vectors(l, bsz, bsz + 1)

    px = _proj_even(x, sh_a, sc_a, gm, consts, cos, sin, True, tm)
    pc = _proj_even(ctx, csh_a, csc_a, gm, consts, cos[:ctx.shape[1]], sin[:ctx.shape[1]], False, tm)
    dq_x, dk_x, dv_x, mq_x, mk_x, mv_x = px
    dq_c, dk_c, dv_c, mq_c, mk_c, mv_c = pc

    da_x = _diff_attn(dq_x, [dk_c, dk_x], [dv_c, dv_x], lqk, out_g, lam_init, tq, nsub)
    mla_x = _mla_attn(mq_x, [mk_c, mk_x], [mv_c, mv_x], tq, nsub)
    da_c = _diff_attn(dq_c, [dk_c], [dv_c], lqk, out_g, lam_init, tq, 1)
    mla_c = _mla_attn(mq_c, [mk_c], [mv_c], tq, 1)

    x = _post(x, [da_x, mla_x], g_a, sh_f, sc_f, g_f, gf, w_os, f_in, f_out, tm)
    ctx = _post(ctx, [da_c, mla_c], cg_a, csh_f, csc_f, cg_f, gf, w_os, f_in, f_out, tm)

    l = 1
    sh_a, sc_a, g_a, sh_f, sc_f, g_f = mod_vectors(l, 0, bsz)
    csh_a, csc_a = mod_vectors(l, bsz, bsz + 1)[:2]
    gm = norm_mix_g[l].reshape(1, d)
    gf = norm_ffn_g[l].reshape(1, d)
    consts = [od_w_in[0].astype(BF16), g64, _tile_lanes(na_q_g[0], NA_HEADS), _tile_lanes(na_k_g[0], NA_HEADS)]
    q_x, k_x, v_x = _proj_odd(x, sh_a, sc_a, gm, consts, tm)
    _, k_c, v_c = _proj_odd(ctx, csh_a, csc_a, gm, consts, tm)
    y = _na_attn(q_x, k_x, v_x, k_c, v_c, _na_bias_table(na_rpb[0]))
    x = _post(x, [y], g_a, sh_f, sc_f, g_f, gf, [w_out[l].astype(BF16)],
              ffn_w_in[l].astype(BF16), ffn_w_out[l].astype(BF16), tm)
    return x
```

```python
import functools
import math

import jax
import jax.numpy as jnp
import numpy as np
from jax import lax
from jax.experimental import pallas as pl
from jax.experimental.pallas import tpu as pltpu

D_MODEL = 1024
DEPTH = 2
GRID_W = 64
DA_HEADS = 4
DA_DK = 64
DA_DV = 2 * DA_DK
MLA_HEADS = 4
MLA_NOPE = 128
MLA_ROPE = 64
MLA_V = 128
MLA_Q_LORA = 256
MLA_KV_LORA = 128
NA_HEADS = 16
NA_DH = 64
NA_KH = 8
NA_KW = 16
D_FF = -(-8 * D_MODEL // (3 * 256)) * 256
ROPE_THETA = 10000.0
EPS = 1e-6
NEG_INF = -1e30
LOG2E = math.log2(math.e)
SEQ = 2048
NA_ROWS = SEQ // GRID_W
NA_GROUP = 4
NA_GROUPS = NA_ROWS // NA_GROUP
NA_WIN_ROWS = NA_GROUP + NA_KH

LANES = 128
MXU_DIM = 256
VMEM_LIMIT = 56 * 1024 * 1024

PIPE_TILES = 4

BF16 = jnp.bfloat16
F32 = jnp.float32


def _dot(a, b):
    return jnp.dot(a, b, preferred_element_type=F32)


def _dot_nt(a, b):
    return lax.dot_general(a, b, (((1,), (1,)), ((), ())), preferred_element_type=F32)


def _group_rsqrt(x, gmat, size):
    x2 = (x * x).astype(BF16)
    w = gmat.shape[0]
    n = x2.shape[1] // w
    sums = [_dot(x2[:, i * w:(i + 1) * w], gmat) for i in range(n)]
    ss = sums[0] if n == 1 else jnp.concatenate(sums, axis=1)
    return lax.rsqrt(ss * (1.0 / size) + EPS)


def _swap16(x):
    n = x.shape[1]
    lane = lax.broadcasted_iota(jnp.int32, (1, n), 1)
    up = pltpu.roll(x, n - 16, 1)
    down = pltpu.roll(x, 16, 1)
    return jnp.where((lane % 32) < 16, up, down)


def _rope(x, cos, sin_signed):
    return x * cos + _swap16(x) * sin_signed


def _modulated_norm(x, g, shift, scale):
    y = x * lax.rsqrt(jnp.mean(x * x, axis=-1, keepdims=True) + EPS)
    return (y * g) * (1.0 + scale) + shift


def _adaln_kernel(cond_ref, w_ref, b_ref, o_ref):
    c = cond_ref[...]
    a = (c * jax.nn.sigmoid(c)).astype(BF16)
    o_ref[0] = _dot(a, w_ref[0].astype(BF16)) + b_ref[0]


def _adaln(cond, mod_w, mod_b):
    rows = cond.shape[0]
    n = mod_w.shape[2]
    tn = 1536
    return pl.pallas_call(
        _adaln_kernel,
        grid=(DEPTH, n // tn),
        in_specs=[
            pl.BlockSpec((rows, D_MODEL), lambda l, j: (0, 0)),
            pl.BlockSpec((1, D_MODEL, tn), lambda l, j: (l, 0, j)),
            pl.BlockSpec((1, 1, tn), lambda l, j: (l, 0, j)),
        ],
        out_specs=pl.BlockSpec((1, rows, tn), lambda l, j: (l, 0, j)),
        out_shape=jax.ShapeDtypeStruct((DEPTH, rows, n), F32),
        compiler_params=pltpu.CompilerParams(
            dimension_semantics=("arbitrary", "arbitrary"), vmem_limit_bytes=VMEM_LIMIT),
        name="adaln",
    )(cond, mod_w, mod_b.reshape(DEPTH, 1, n))


def _proj_even_kernel(rope, x_ref, sh_ref, sc_ref, g_ref, w_ref, g64_ref, g128_ref, g256_ref,
                      dqg_ref, dkg_ref, qag_ref, wuq_ref, kvag_ref, wukv_ref, mqg_ref, mkg_ref,
                      krg_ref, cos_ref, sin_ref,
                      dq_ref, dk_ref, dv_ref, mq_ref, mk_ref, mv_ref):
    h = _modulated_norm(x_ref[0], g_ref[...], sh_ref[0], sc_ref[0]).astype(BF16)
    p = _dot(h, w_ref[...])
    n_dq = DA_HEADS * 2 * DA_DK
    if rope:
        cos128 = cos_ref[...]
        sin128 = sin_ref[...]
        cos512 = jnp.concatenate([cos128] * 4, axis=1)
        sin512 = jnp.concatenate([sin128] * 4, axis=1)
        lane = lax.broadcasted_iota(jnp.int32, (1, LANES), 1)
        cos_half = jnp.where(lane < MLA_ROPE, cos128, 1.0)
        sin_half = jnp.where(lane < MLA_ROPE, sin128, 0.0)

    def da_qk(v, gain, scale):
        y = v * _group_rsqrt(v, g64_ref[...], DA_DK) * (gain * scale)
        if rope:
            y = _rope(y, cos512, sin512)
        return y.astype(BF16)

    dq_ref[0] = da_qk(p[:, 0:n_dq], dqg_ref[...], DA_DK ** -0.5 * LOG2E)
    dk_ref[0] = da_qk(p[:, n_dq:2 * n_dq], dkg_ref[...], 1.0)
    ones = jnp.ones((p.shape[0], LANES), BF16)
    dv = p[:, 2 * n_dq:3 * n_dq].astype(BF16)
    dv_ref[0] = jnp.concatenate(
        [blk for hh in range(DA_HEADS) for blk in (dv[:, hh * DA_DV:(hh + 1) * DA_DV], ones)], axis=1)

    c0 = 3 * n_dq
    cq = p[:, c0:c0 + MLA_Q_LORA]
    cq = cq * _group_rsqrt(cq, g256_ref[...], MLA_Q_LORA) * qag_ref[...]
    q = _dot(cq.astype(BF16), wuq_ref[...])
    q = q * _group_rsqrt(q, g256_ref[...], MLA_NOPE + MLA_ROPE) * (
        mqg_ref[...] * ((MLA_NOPE + MLA_ROPE) ** -0.5 * LOG2E))

    c1 = c0 + MLA_Q_LORA
    ckv = p[:, c1:c1 + MLA_KV_LORA]
    ckv = ckv * _group_rsqrt(ckv, g128_ref[...], MLA_KV_LORA) * kvag_ref[...]
    kv = _dot(ckv.astype(BF16), wukv_ref[...])

    c2 = c1 + MLA_KV_LORA
    kr = p[:, c2:c2 + LANES]
    kr = kr * _group_rsqrt(kr, g128_ref[...], MLA_ROPE) * krg_ref[...]
    if rope:
        kr = _rope(kr, cos_half, sin_half)
    kr = kr.astype(BF16)

    mq, mk, mv = [], [], []
    for hh in range(MLA_HEADS):
        b0 = hh * 2 * LANES
        qn = q[:, b0:b0 + LANES]
        qr = q[:, b0 + LANES:b0 + 2 * LANES]
        if rope:
            qr = _rope(qr, cos_half, sin_half)
        mq += [qn.astype(BF16), qr.astype(BF16)]
        kn = kv[:, b0:b0 + LANES]
        kn = kn * _group_rsqrt(kn, g128_ref[...], MLA_NOPE) * mkg_ref[...]
        mk += [kn.astype(BF16), kr]
        mv += [kv[:, b0 + LANES:b0 + 2 * LANES].astype(BF16), ones]
    mq_ref[0] = jnp.concatenate(mq, axis=1)
    mk_ref[0] = jnp.concatenate(mk, axis=1)
    mv_ref[0] = jnp.concatenate(mv, axis=1)


def _const_spec(a):
    nd = a.ndim
    return pl.BlockSpec(a.shape, lambda b, i: (0,) * nd)


def _mod_spec(a):
    if a.shape[0] == 1:
        return pl.BlockSpec((1, 1, a.shape[2]), lambda b, i: (0, 0, 0))
    return pl.BlockSpec((1, 1, a.shape[2]), lambda b, i: (b, 0, 0))


def _proj_even(x, shift, scale, gain, consts, cos, sin, rope, tm):
    bn, t, d = x.shape
    tok = lambda w: pl.BlockSpec((1, tm, w), lambda b, i: (b, i, 0))
    rope_spec = pl.BlockSpec((tm, LANES), lambda b, i: (i, 0))
    widths = (512, 512, 1024, 1024, 1024, 1024)
    return pl.pallas_call(
        functools.partial(_proj_even_kernel, rope),
        grid=(bn, t // tm),
        in_specs=[tok(d), _mod_spec(shift), _mod_spec(scale), _const_spec(gain)]
                 + [_const_spec(c) for c in consts] + [rope_spec, rope_spec],
        out_specs=[tok(w) for w in widths],
        out_shape=[jax.ShapeDtypeStruct((bn, t, w), BF16) for w in widths],
        compiler_params=pltpu.CompilerParams(
            dimension_semantics=("arbitrary", "arbitrary"), vmem_limit_bytes=VMEM_LIMIT),
        name="proj_even",
    )(x, shift, scale, gain, *consts, cos, sin)


def _softmax_parts(s_list):
    m = functools.reduce(jnp.maximum, [jnp.max(s, axis=-1, keepdims=True) for s in s_list])
    e_list = [jnp.exp2(s - m) for s in s_list]
    l = functools.reduce(jnp.add, [jnp.sum(e, axis=-1, keepdims=True) for e in e_list])
    return e_list, l


def _diff_attn_kernel(nseg, nsub, lam_init, *refs):
    q_ref = refs[0]
    k_refs = refs[1:1 + nseg]
    v_refs = refs[1 + nseg:1 + 2 * nseg]
    lqk_ref, og_ref, o_ref = refs[1 + 2 * nseg:]
    lqk = lqk_ref[...]
    lam = (jnp.exp(jnp.sum(lqk[0:1] * lqk[1:2], axis=-1, keepdims=True))
           - jnp.exp(jnp.sum(lqk[2:3] * lqk[3:4], axis=-1, keepdims=True)) + lam_init)
    lane = lax.broadcasted_iota(jnp.int32, (1, LANES), 1)
    rows = q_ref.shape[1] // nsub
    for t in range(nsub):
        sl = slice(t * rows, (t + 1) * rows)
        q = q_ref[0, sl, :]
        zero = jnp.zeros_like(q)
        q1 = jnp.where(lane < DA_DK, q, zero)
        q2 = jnp.where(lane >= DA_DK, q, zero)
        e1, l1 = _softmax_parts([_dot_nt(q1, k[0]) for k in k_refs])
        e2, l2 = _softmax_parts([_dot_nt(q2, k[0]) for k in k_refs])
        r1 = 1.0 / l1
        r2 = lam / l2
        o = functools.reduce(jnp.add, [_dot((a * r1 - b * r2).astype(BF16), v[0])
                                       for a, b, v in zip(e1, e2, v_refs)])
        o = o * lax.rsqrt(jnp.mean(o * o, axis=-1, keepdims=True) + EPS) * (og_ref[...] * (1.0 - lam_init))
        o_ref[0, sl, :] = o.astype(BF16)


def _diff_attn(q, ks, vs, lqk, out_g, lam_init, tq, nsub):
    bn, t, _ = q.shape
    nseg = len(ks)
    tq = min(tq, t)
    qspec = pl.BlockSpec((1, tq, LANES), lambda b, h, i: (b, i, h))
    kspec = lambda a: pl.BlockSpec((1, a.shape[1], LANES), lambda b, h, i: (b, 0, h))
    vspec = lambda a: pl.BlockSpec((1, a.shape[1], LANES), lambda b, h, i: (b, 0, 2 * h))
    cspec = lambda a: pl.BlockSpec(a.shape, lambda b, h, i: (0, 0))
    return pl.pallas_call(
        functools.partial(_diff_attn_kernel, nseg, nsub, lam_init),
        grid=(bn, DA_HEADS, t // tq),
        in_specs=[qspec] + [kspec(k) for k in ks] + [vspec(v) for v in vs] + [cspec(lqk), cspec(out_g)],
        out_specs=qspec,
        out_shape=jax.ShapeDtypeStruct((bn, t, DA_HEADS * DA_DV), BF16),
        compiler_params=pltpu.CompilerParams(
            dimension_semantics=("arbitrary",) * 3, vmem_limit_bytes=VMEM_LIMIT),
        name="diff_attn",
    )(q, *ks, *vs, lqk, out_g)


def _mla_attn_kernel(nseg, nsub, *refs):
    q_ref = refs[0]
    k_refs = refs[1:1 + nseg]
    v_refs = refs[1 + nseg:1 + 2 * nseg]
    o_ref = refs[1 + 2 * nseg]
    rows = q_ref.shape[1] // nsub
    for t in range(nsub):
        sl = slice(t * rows, (t + 1) * rows)
        q = q_ref[0, sl, :]
        e, l = _softmax_parts([_dot_nt(q, k[0]) for k in k_refs])
        o = functools.reduce(jnp.add, [_dot(a.astype(BF16), v[0]) for a, v in zip(e, v_refs)])
        o_ref[0, sl, :] = (o * (1.0 / l)).astype(BF16)


def _mla_attn(q, ks, vs, tq, nsub):
    bn, t, _ = q.shape
    nseg = len(ks)
    tq = min(tq, t)
    hw = 2 * LANES
    qspec = pl.BlockSpec((1, tq, hw), lambda b, h, i: (b, i, h))
    kspec = lambda a: pl.BlockSpec((1, a.shape[1], hw), lambda b, h, i: (b, 0, h))
    vspec = lambda a: pl.BlockSpec((1, a.shape[1], LANES), lambda b, h, i: (b, 0, 2 * h))
    return pl.pallas_call(
        functools.partial(_mla_attn_kernel, nseg, nsub),
        grid=(bn, MLA_HEADS, t // tq),
        in_specs=[qspec] + [kspec(k) for k in ks] + [vspec(v) for v in vs],
        out_specs=pl.BlockSpec((1, tq, LANES), lambda b, h, i: (b, i, h)),
        out_shape=jax.ShapeDtypeStruct((bn, t, MLA_HEADS * MLA_V), BF16),
        compiler_params=pltpu.CompilerParams(
            dimension_semantics=("arbitrary",) * 3, vmem_limit_bytes=VMEM_LIMIT),
        name="mla_attn",
    )(q, *ks, *vs)


def _next_step_maps(nb, nh, ns):
    last = nb * nh * ns - 1

    def nxt(b, h, i):
        u = jnp.minimum((b * nh + h) * ns + i + 1, last)
        return u // (nh * ns), (u // ns) % nh, u % ns

    return nxt


def _is_first_step():
    return (pl.program_id(0) == 0) & (pl.program_id(1) == 0) & (pl.program_id(2) == 0)


def _lane_bcast(col, rows):
    return jnp.broadcast_to(col, (rows, LANES))


def _attn_pipe_kernel(diff, lam_init, *refs):
    n_in = 10 if diff else 8
    q_ref, qn_ref, kc_ref, kx_ref, kcn_ref, kxn_ref, vc_ref, vx_ref = refs[:8]
    o_ref = refs[n_in]
    s_refs = refs[n_in + 1:n_in + 1 + PIPE_TILES]
    m_refs = refs[n_in + 1 + PIPE_TILES:]
    rows = s_refs[0].shape[0]
    tq = rows // 2 if diff else rows
    nc = kc_ref.shape[1]
    nk = s_refs[0].shape[1]
    lane = lax.broadcasted_iota(jnp.int32, (1, LANES), 1)

    def scores(q, kc, kx, s_ref, m_ref):
        if diff:
            zero = jnp.zeros_like(q)
            q = jnp.concatenate([jnp.where(lane < DA_DK, q, zero), jnp.where(lane < DA_DK, zero, q)], axis=0)
        sc = _dot_nt(q, kc[0])
        sx = _dot_nt(q, kx[0])
        s_ref[:, :nc] = sc
        s_ref[:, nc:] = sx
        m = jnp.maximum(jnp.max(sc, axis=-1, keepdims=True), jnp.max(sx, axis=-1, keepdims=True))
        m_ref[...] = _lane_bcast(m, rows)

    def attend(s_ref, m_ref, out_rows):
        m = jnp.concatenate([m_ref[...]] * 2, axis=1)
        acc = None
        for j in range(nk // MXU_DIM):
            k0 = j * MXU_DIM
            p = jnp.exp2(s_ref[:, k0:k0 + MXU_DIM] - m).astype(BF16)
            v = vc_ref[0, k0:k0 + MXU_DIM, :] if k0 < nc else vx_ref[0, k0 - nc:k0 - nc + MXU_DIM, :]
            part = _dot(p, v)
            acc = part if acc is None else acc + part
        o = acc[:, :LANES] * (1.0 / acc[:, LANES:])
        if diff:
            lqk_ref, og_ref = refs[8:10]
            lqk = lqk_ref[...]
            lam = (jnp.exp(jnp.sum(lqk[0:1] * lqk[1:2], axis=-1, keepdims=True))
                   - jnp.exp(jnp.sum(lqk[2:3] * lqk[3:4], axis=-1, keepdims=True)) + lam_init)
            o = o[:tq] - lam * o[tq:]
            o = o * lax.rsqrt(jnp.mean(o * o, axis=-1, keepdims=True) + EPS) * (og_ref[...] * (1.0 - lam_init))
        o_ref[0, out_rows, :] = o.astype(BF16)

    def tile_rows(t):
        return slice(t * tq, (t + 1) * tq)

    @pl.when(_is_first_step())
    def _():
        for t in range(2):
            scores(q_ref[0, tile_rows(t), :], kc_ref, kx_ref, s_refs[t], m_refs[t])

    for t in range(PIPE_TILES):
        attend(s_refs[t], m_refs[t], tile_rows(t))
        ahead = t + 2
        if ahead < PIPE_TILES:
            scores(q_ref[0, tile_rows(ahead), :], kc_ref, kx_ref, s_refs[ahead], m_refs[ahead])
        else:
            ahead -= PIPE_TILES
            scores(qn_ref[0, tile_rows(ahead), :], kcn_ref, kxn_ref, s_refs[ahead], m_refs[ahead])


def _attn_pipe(q, kc, kx, vc, vx, tq, diff_params=None):
    diff = diff_params is not None
    bn, t, _ = q.shape
    nh = DA_HEADS if diff else MLA_HEADS
    hw = LANES if diff else 2 * LANES
    step_rows = PIPE_TILES * tq
    ns = t // step_rows
    assert t % step_rows == 0
    rows = 2 * tq if diff else tq
    nkeys = kc.shape[1] + kx.shape[1]
    nxt = _next_step_maps(bn, nh, ns)

    def q_next(b, h, i):
        b2, h2, i2 = nxt(b, h, i)
        return b2, i2, h2

    def k_next(b, h, i):
        b2, h2, _ = nxt(b, h, i)
        return b2, 0, h2

    cur = lambda b, h, i: (b, 0, h)
    cspec = lambda a: pl.BlockSpec(a.shape, lambda b, h, i: (0, 0))
    extra = list(diff_params[:2]) if diff else []
    lam_init = diff_params[2] if diff else 0.0
    return pl.pallas_call(
        functools.partial(_attn_pipe_kernel, diff, lam_init),
        grid=(bn, nh, ns),
        in_specs=[pl.BlockSpec((1, step_rows, hw), lambda b, h, i: (b, i, h)),
                  pl.BlockSpec((1, step_rows, hw), q_next),
                  pl.BlockSpec((1, kc.shape[1], hw), cur),
                  pl.BlockSpec((1, kx.shape[1], hw), cur),
                  pl.BlockSpec((1, kc.shape[1], hw), k_next),
                  pl.BlockSpec((1, kx.shape[1], hw), k_next),
                  pl.BlockSpec((1, vc.shape[1], 2 * LANES), cur),
                  pl.BlockSpec((1, vx.shape[1], 2 * LANES), cur)]
                 + [cspec(a) for a in extra],
        out_specs=pl.BlockSpec((1, step_rows, LANES), lambda b, h, i: (b, i, h)),
        out_shape=jax.ShapeDtypeStruct((bn, t, nh * LANES), BF16),
        scratch_shapes=[pltpu.VMEM((rows, nkeys), F32)] * PIPE_TILES + [pltpu.VMEM((rows, LANES), F32)] * PIPE_TILES,
        compiler_params=pltpu.CompilerParams(
            dimension_semantics=("arbitrary",) * 3, vmem_limit_bytes=VMEM_LIMIT),
        name="diff_pipe" if diff else "mla_pipe",
    )(q, q, kc, kx, kc, kx, vc, vx, *extra)


def _post_kernel(ny, *refs):
    x_ref = refs[0]
    y_refs = refs[1:1 + ny]
    ga_ref, shf_ref, scf_ref, gf_ref, g_ref = refs[1 + ny:6 + ny]
    wo_refs = refs[6 + ny:6 + 2 * ny]
    win_ref, wout_ref, o_ref = refs[6 + 2 * ny:]
    attn = functools.reduce(jnp.add, [_dot(y[0], w[...]) for y, w in zip(y_refs, wo_refs)])
    x1 = x_ref[0] + ga_ref[0] * attn
    h = _modulated_norm(x1, g_ref[...], shf_ref[0], scf_ref[0]).astype(BF16)
    u = _dot(h, win_ref[...])
    gate = u[:, :D_FF]
    act = (gate * jax.nn.sigmoid(gate) * u[:, D_FF:]).astype(BF16)
    o_ref[0] = x1 + gf_ref[0] * _dot(act, wout_ref[...])


def _post(x, ys, gate_a, shift_f, scale_f, gate_f, gain, w_os, w_in, w_out, tm):
    bn, t, d = x.shape
    tok = lambda w: pl.BlockSpec((1, tm, w), lambda b, i: (b, i, 0))
    wspec = lambda a: pl.BlockSpec(a.shape, lambda b, i: (0, 0), pipeline_mode=pl.Buffered(1))
    return pl.pallas_call(
        functools.partial(_post_kernel, len(ys)),
        grid=(bn, t // tm),
        in_specs=[tok(d)] + [tok(y.shape[2]) for y in ys]
                 + [_mod_spec(gate_a), _mod_spec(shift_f), _mod_spec(scale_f), _mod_spec(gate_f), _const_spec(gain)]
                 + [wspec(w) for w in w_os] + [wspec(w_in), wspec(w_out)],
        out_specs=tok(d),
        out_shape=jax.ShapeDtypeStruct((bn, t, d), F32),
        compiler_params=pltpu.CompilerParams(
            dimension_semantics=("arbitrary", "arbitrary"), vmem_limit_bytes=VMEM_LIMIT),
        name="post",
    )(x, *ys, gate_a, shift_f, scale_f, gate_f, gain, *w_os, w_in, w_out)


def _proj_odd_kernel(x_ref, sh_ref, sc_ref, g_ref, w_ref, g64_ref, qg_ref, kg_ref, q_ref, k_ref, v_ref):
    h = _modulated_norm(x_ref[0], g_ref[...], sh_ref[0], sc_ref[0]).astype(BF16)
    p = _dot(h, w_ref[...])
    w = NA_HEADS * NA_DH

    def qk(v, gain, scale):
        return (v * _group_rsqrt(v, g64_ref[...], NA_DH) * (gain * scale)).astype(BF16)

    q_ref[0] = qk(p[:, :w], qg_ref[...], NA_DH ** -0.5 * LOG2E)
    k_ref[0] = qk(p[:, w:2 * w], kg_ref[...], 1.0)
    v = p[:, 2 * w:].astype(BF16)
    ones = jnp.ones((p.shape[0], LANES), BF16)
    v_ref[0] = jnp.concatenate(
        [blk for j in range(w // LANES) for blk in (v[:, j * LANES:(j + 1) * LANES], ones)], axis=1)


def _proj_odd(x, shift, scale, gain, consts, tm):
    bn, t, d = x.shape
    tok = lambda w: pl.BlockSpec((1, tm, w), lambda b, i: (b, i, 0))
    w = NA_HEADS * NA_DH
    widths = (w, w, 2 * w)
    return pl.pallas_call(
        _proj_odd_kernel,
        grid=(bn, t // tm),
        in_specs=[tok(d), _mod_spec(shift), _mod_spec(scale), _const_spec(gain)] + [_const_spec(c) for c in consts],
        out_specs=[tok(n) for n in widths],
        out_shape=[jax.ShapeDtypeStruct((bn, t, n), BF16) for n in widths],
        compiler_params=pltpu.CompilerParams(
            dimension_semantics=("arbitrary", "arbitrary"), vmem_limit_bytes=VMEM_LIMIT),
        name="proj_odd",
    )(x, shift, scale, gain, *consts)


def _na_window_start(g):
    return min(max(NA_GROUP * g - NA_KH // 2, 0), NA_ROWS - NA_WIN_ROWS)


def _na_bias_variant(g):
    return {0: 0, 1: 1, NA_GROUPS - 1: 3}.get(g, 2)


def _na_kernel(q_ref, k_ref, v_ref, kc_ref, vc_ref, bias_ref, qn_ref, kn_ref, kcn_ref, biasn_ref, o_ref,
               *scratch):
    s_refs = scratch[:PIPE_TILES]
    m_refs = scratch[PIPE_TILES:]
    lane = lax.broadcasted_iota(jnp.int32, (1, LANES), 1)
    lo = lane < NA_DH
    gq = NA_GROUP * GRID_W
    nloc = NA_WIN_ROWS * GRID_W
    nk = s_refs[0].shape[1]

    def win(g):
        k0 = _na_window_start(g) * GRID_W
        return slice(k0, k0 + nloc)

    def scores(g, q_r, k_r, kc_r, bias_r, s_ref, m_ref):
        q = q_r[0, g * gq:(g + 1) * gq, :]
        zero = jnp.zeros_like(q)
        qq = jnp.concatenate([jnp.where(lo, q, zero), jnp.where(lo, zero, q)], axis=0)
        s_loc = _dot_nt(qq, k_r[0, win(g), :]) + bias_r[0, _na_bias_variant(g)]
        s_ctx = _dot_nt(qq, kc_r[0])
        s_ref[:, :nloc] = s_loc
        s_ref[:, nloc:] = s_ctx
        m = jnp.maximum(jnp.max(s_loc, axis=-1, keepdims=True), jnp.max(s_ctx, axis=-1, keepdims=True))
        m_ref[...] = _lane_bcast(m, 2 * gq)

    def attend(g, s_ref, m_ref):
        m = jnp.concatenate([m_ref[...]] * 2, axis=1)
        k0 = _na_window_start(g) * GRID_W
        acc = None
        for j in range(nk // MXU_DIM):
            c0 = j * MXU_DIM
            p = jnp.exp2(s_ref[:, c0:c0 + MXU_DIM] - m).astype(BF16)
            v = v_ref[0, k0 + c0:k0 + c0 + MXU_DIM, :] if c0 < nloc else vc_ref[0, c0 - nloc:c0 - nloc + MXU_DIM, :]
            part = _dot(p, v)
            acc = part if acc is None else acc + part
        o = acc[:, :LANES] * (1.0 / acc[:, LANES:])
        o_ref[0, g * gq:(g + 1) * gq, :] = jnp.where(lo, o[:gq], o[gq:]).astype(BF16)

    @pl.when((pl.program_id(0) == 0) & (pl.program_id(1) == 0))
    def _():
        for g in range(2):
            scores(g, q_ref, k_ref, kc_ref, bias_ref, s_refs[g], m_refs[g])

    for g in range(NA_GROUPS):
        attend(g, s_refs[g % PIPE_TILES], m_refs[g % PIPE_TILES])
        ahead = g + 2
        slot = ahead % PIPE_TILES
        if ahead < NA_GROUPS:
            scores(ahead, q_ref, k_ref, kc_ref, bias_ref, s_refs[slot], m_refs[slot])
        else:
            scores(ahead - NA_GROUPS, qn_ref, kn_ref, kcn_ref, biasn_ref, s_refs[slot], m_refs[slot])


def _na_attn(q, k, v, kc, vc, bias):
    bn, s, w = q.shape
    assert s == NA_ROWS * GRID_W and NA_GROUPS % PIPE_TILES == 0
    pairs = w // LANES
    last = pairs * bn - 1

    def nxt(p, b):
        u = jnp.minimum(p * bn + b + 1, last)
        return u % bn, 0, u // bn

    cur = lambda p, b: (b, 0, p)
    spec = lambda a, width, imap: pl.BlockSpec((1, a.shape[1], width), imap)
    rows = 2 * NA_GROUP * GRID_W
    nk = NA_WIN_ROWS * GRID_W + kc.shape[1]
    bias_head = bias[:, :2]
    return pl.pallas_call(
        _na_kernel,
        grid=(pairs, bn),
        in_specs=[spec(q, LANES, cur), spec(k, LANES, cur), spec(v, 2 * LANES, cur),
                  spec(kc, LANES, cur), spec(vc, 2 * LANES, cur),
                  pl.BlockSpec((1,) + bias.shape[1:], lambda p, b: (p, 0, 0, 0)),
                  spec(q, LANES, nxt), spec(k, LANES, nxt), spec(kc, LANES, nxt),
                  pl.BlockSpec((1,) + bias_head.shape[1:], lambda p, b: (nxt(p, b)[2], 0, 0, 0))],
        out_specs=spec(q, LANES, cur),
        out_shape=jax.ShapeDtypeStruct((bn, s, w), BF16),
        scratch_shapes=[pltpu.VMEM((rows, nk), F32)] * PIPE_TILES + [pltpu.VMEM((rows, LANES), F32)] * PIPE_TILES,
        compiler_params=pltpu.CompilerParams(
            dimension_semantics=("arbitrary", "arbitrary"), vmem_limit_bytes=VMEM_LIMIT),
        name="na_attn",
    )(q, k, v, kc, vc, bias, q, k, kc, bias_head)


def _group_matrix(width, group):
    idx = np.arange(width) // group
    return jnp.asarray((idx[:, None] == idx[None, :]).astype(np.float32), dtype=BF16)


def _rope_tables(s):
    t = np.arange(s)
    row = (t // GRID_W).astype(np.float32)
    col = (t % GRID_W).astype(np.float32)
    half = DA_DK // 2
    inv = jnp.asarray(ROPE_THETA, F32) ** (-jnp.arange(0, half, 2, dtype=F32) / half)
    ar = jnp.asarray(row)[:, None] * inv
    ac = jnp.asarray(col)[:, None] * inv
    cr, sr, cc, sc = jnp.cos(ar), jnp.sin(ar), jnp.cos(ac), jnp.sin(ac)
    cos64 = jnp.concatenate([cr, cr, cc, cc], axis=1)
    sin64 = jnp.concatenate([-sr, sr, -sc, sc], axis=1)
    return jnp.concatenate([cos64, cos64], axis=1), jnp.concatenate([sin64, sin64], axis=1)


def _na_bias_table(rpb):
    cols = np.arange(GRID_W)
    col_start = np.clip(cols - NA_KW // 2, 0, GRID_W - NA_KW)
    col_ok = (cols[None, :] >= col_start[:, None]) & (cols[None, :] < col_start[:, None] + NA_KW)
    dc_idx = np.clip(cols[None, :] - cols[:, None], -(NA_KW - 1), NA_KW - 1) + NA_KW - 1
    tables = []
    for g in (0, 1, 2, NA_GROUPS - 1):
        r = NA_GROUP * g + np.arange(NA_GROUP)
        kr = _na_window_start(g) + np.arange(NA_WIN_ROWS)
        rs = np.clip(r - NA_KH // 2, 0, NA_ROWS - NA_KH)
        row_ok = (kr[None, :] >= rs[:, None]) & (kr[None, :] < rs[:, None] + NA_KH)
        dr_idx = np.clip(kr[None, :] - r[:, None] + NA_KH - 1, 0, 2 * NA_KH - 2)
        ok = row_ok[:, None, :, None] & col_ok[None, :, None, :]
        vals = rpb[:, dr_idx[:, None, :, None], dc_idx[None, :, None, :]]
        t = jnp.where(jnp.asarray(ok)[None], vals * LOG2E, NEG_INF)
        tables.append(t.reshape(NA_HEADS // 2, 2 * NA_GROUP * GRID_W, NA_WIN_ROWS * GRID_W))
    return jnp.stack(tables, axis=1)


def _tile_lanes(g, n):
    return jnp.tile(g, n).reshape(1, -1)


def kernel(x, c, ctx, c_ctx, mod_w, mod_b, norm_mix_g, norm_ffn_g, w_out, ffn_w_in, ffn_w_out, ev_w_in, da_q_g, da_k_g, da_lq1, da_lk1, da_lq2, da_lk2, da_out_g, mla_q_a_g, mla_w_uq, mla_kv_a_g, mla_w_ukv, mla_q_g, mla_k_g, mla_kr_g, od_w_in, na_q_g, na_k_g, na_rpb):
    bsz, seq, d = x.shape
    tm = 256
    tq, nsub = 512, 2

    cond = jnp.concatenate([c, c_ctx[None], jnp.zeros((7, d), F32)], axis=0)
    mods = _adaln(cond, mod_w, mod_b)

    def mod_vectors(l, lo, hi):
        return [mods[l, lo:hi, j * d:(j + 1) * d].reshape(hi - lo, 1, d) for j in range(6)]

    g64 = _group_matrix(MXU_DIM, 64)
    cos, sin = _rope_tables(seq)

    l = 0
    w_in = jnp.concatenate([ev_w_in[0], jnp.zeros((d, 64), F32)], axis=1).astype(BF16)
    wuq = mla_w_uq[0].reshape(MLA_Q_LORA, MLA_HEADS, MLA_NOPE + MLA_ROPE)
    wuq = jnp.pad(wuq, ((0, 0), (0, 0), (0, 64))).reshape(MLA_Q_LORA, MLA_HEADS * 256).astype(BF16)
    mqg = jnp.tile(jnp.pad(mla_q_g[0], (0, 64)), MLA_HEADS).reshape(1, -1)
    consts = [
        w_in, g64, _group_matrix(LANES, LANES), _group_matrix(MXU_DIM, MXU_DIM),
        _tile_lanes(da_q_g[0], 8), _tile_lanes(da_k_g[0], 8),
        mla_q_a_g[0].reshape(1, -1), wuq, mla_kv_a_g[0].reshape(1, -1), mla_w_ukv[0].astype(BF16),
        mqg, mla_k_g[0].reshape(1, -1), jnp.pad(mla_kr_g[0], (0, 64)).reshape(1, -1),
    ]
    lqk = jnp.stack([da_lq1[0], da_lk1[0], da_lq2[0], da_lk2[0]])
    lam_init = 0.8 - 0.6 * math.exp(-0.3 * l)
    out_g = da_out_g[0].reshape(1, -1)
    w_o = w_out[l].astype(BF16)
    w_os = [w_o[:DA_HEADS * DA_DV], w_o[DA_HEADS * DA_DV:]]
    f_in = ffn_w_in[l].astype(BF16)
    f_out = ffn_w_out[l].astype(BF16)
    gm = norm_mix_g[l].reshape(1, d)
    gf = norm_ffn_g[l].reshape(1, d)

    sh_a, sc_a, g_a, sh_f, sc_f, g_f = mod_vectors(l, 0, bsz)
    csh_a, csc_a, cg_a, csh_f, csc_f, cg_f = mod_vectors(l, bsz, bsz + 1)

    px = _proj_even(x, sh_a, sc_a, gm, consts, cos, sin, True, tm)
    pc = _proj_even(ctx, csh_a, csc_a, gm, consts, cos[:ctx.shape[1]], sin[:ctx.shape[1]], False, tm)
    dq_x, dk_x, dv_x, mq_x, mk_x, mv_x = px
    dq_c, dk_c, dv_c, mq_c, mk_c, mv_c = pc

    da_x = _attn_pipe(dq_x, dk_c, dk_x, dv_c, dv_x, 256, (lqk, out_g, lam_init))
    mla_x = _attn_pipe(mq_x, mk_c, mk_x, mv_c, mv_x, 256)
    da_c = _diff_attn(dq_c, [dk_c], [dv_c], lqk, out_g, lam_init, tq, 1)
    mla_c = _mla_attn(mq_c, [mk_c], [mv_c], tq, 1)

    x = _post(x, [da_x, mla_x], g_a, sh_f, sc_f, g_f, gf, w_os, f_in, f_out, tm)
    ctx = _post(ctx, [da_c, mla_c], cg_a, csh_f, csc_f, cg_f, gf, w_os, f_in, f_out, tm)

    l = 1
    sh_a, sc_a, g_a, sh_f, sc_f, g_f = mod_vectors(l, 0, bsz)
    csh_a, csc_a = mod_vectors(l, bsz, bsz + 1)[:2]
    gm = norm_mix_g[l].reshape(1, d)
    gf = norm_ffn_g[l].reshape(1, d)
    consts = [od_w_in[0].astype(BF16), g64, _tile_lanes(na_q_g[0], NA_HEADS), _tile_lanes(na_k_g[0], NA_HEADS)]
    q_x, k_x, v_x = _proj_odd(x, sh_a, sc_a, gm, consts, tm)
    _, k_c, v_c = _proj_odd(ctx, csh_a, csc_a, gm, consts, tm)
    y = _na_attn(q_x, k_x, v_x, k_c, v_c, _na_bias_table(na_rpb[0]))
    x = _post(x, [y], g_a, sh_f, sc_f, g_f, gf, [w_out[l].astype(BF16)],
              ffn_w_in[l].astype(BF16), ffn_w_out[l].astype(BF16), tm)
    return x
```

```python
import functools
import math

import jax
import jax.numpy as jnp
import numpy as np
from jax import lax
from jax.experimental import pallas as pl
from jax.experimental.pallas import tpu as pltpu

D_MODEL = 1024
DEPTH = 2
GRID_W = 64
DA_HEADS = 4
DA_DK = 64
DA_DV = 2 * DA_DK
MLA_HEADS = 4
MLA_NOPE = 128
MLA_ROPE = 64
MLA_V = 128
MLA_Q_LORA = 256
MLA_KV_LORA = 128
NA_HEADS = 16
NA_DH = 64
NA_KH = 8
NA_KW = 16
D_FF = -(-8 * D_MODEL // (3 * 256)) * 256
ROPE_THETA = 10000.0
EPS = 1e-6
NEG_INF = -1e30
LOG2E = math.log2(math.e)
SEQ = 2048
NA_ROWS = SEQ // GRID_W
NA_GROUP = 4
NA_GROUPS = NA_ROWS // NA_GROUP
NA_WIN_ROWS = NA_GROUP + NA_KH

LANES = 128
MXU_DIM = 256
VMEM_LIMIT = 56 * 1024 * 1024

PROJ_ROWS = 256
POST_ROWS = 256
PIPE_TILES = 4

BF16 = jnp.bfloat16
F32 = jnp.float32


def _dot(a, b):
    return jnp.dot(a, b, preferred_element_type=F32)


def _dot_nt(a, b):
    return lax.dot_general(a, b, (((1,), (1,)), ((), ())), preferred_element_type=F32)


def _group_rsqrt(x, gmat, size):
    x2 = (x * x).astype(BF16)
    w = gmat.shape[0]
    n = x2.shape[1] // w
    sums = [_dot(x2[:, i * w:(i + 1) * w], gmat) for i in range(n)]
    ss = sums[0] if n == 1 else jnp.concatenate(sums, axis=1)
    return lax.rsqrt(ss * (1.0 / size) + EPS)


def _swap16(x):
    n = x.shape[1]
    lane = lax.broadcasted_iota(jnp.int32, (1, n), 1)
    up = pltpu.roll(x, n - 16, 1)
    down = pltpu.roll(x, 16, 1)
    return jnp.where((lane % 32) < 16, up, down)


def _rope(x, cos, sin_signed):
    return x * cos + _swap16(x) * sin_signed


def _modulated_norm(x, g, shift, scale):
    y = x * lax.rsqrt(jnp.mean(x * x, axis=-1, keepdims=True) + EPS)
    return (y * g) * (1.0 + scale) + shift


def _adaln_kernel(cond_ref, w_ref, b_ref, o_ref):
    c = cond_ref[...]
    a = (c * jax.nn.sigmoid(c)).astype(BF16)
    o_ref[0] = _dot(a, w_ref[0].astype(BF16)) + b_ref[0]


def _adaln(cond, mod_w, mod_b):
    rows = cond.shape[0]
    n = mod_w.shape[2]
    tn = 1536
    return pl.pallas_call(
        _adaln_kernel,
        grid=(DEPTH, n // tn),
        in_specs=[
            pl.BlockSpec((rows, D_MODEL), lambda l, j: (0, 0)),
            pl.BlockSpec((1, D_MODEL, tn), lambda l, j: (l, 0, j)),
            pl.BlockSpec((1, 1, tn), lambda l, j: (l, 0, j)),
        ],
        out_specs=pl.BlockSpec((1, rows, tn), lambda l, j: (l, 0, j)),
        out_shape=jax.ShapeDtypeStruct((DEPTH, rows, n), F32),
        compiler_params=pltpu.CompilerParams(
            dimension_semantics=("arbitrary", "arbitrary"), vmem_limit_bytes=VMEM_LIMIT),
        name="adaln",
    )(cond, mod_w, mod_b.reshape(DEPTH, 1, n))


def _proj_even_kernel(rope, *refs):
    for t in range(refs[0].shape[1] // PROJ_ROWS):
        _proj_even_tile(rope, slice(t * PROJ_ROWS, (t + 1) * PROJ_ROWS), *refs)


def _proj_even_tile(rope, rs, x_ref, sh_ref, sc_ref, g_ref, w_ref, g64_ref, g128_ref, g256_ref,
                    dqg_ref, dkg_ref, qag_ref, wuq_ref, kvag_ref, wukv_ref, mqg_ref, mkg_ref,
                    krg_ref, cos_ref, sin_ref,
                    dq_ref, dk_ref, dv_ref, mq_ref, mk_ref, mv_ref):
    h = _modulated_norm(x_ref[0, rs, :], g_ref[...], sh_ref[0], sc_ref[0]).astype(BF16)
    p = _dot(h, w_ref[...])
    n_dq = DA_HEADS * 2 * DA_DK
    if rope:
        cos128 = cos_ref[rs, :]
        sin128 = sin_ref[rs, :]
        cos512 = jnp.concatenate([cos128] * 4, axis=1)
        sin512 = jnp.concatenate([sin128] * 4, axis=1)
        lane = lax.broadcasted_iota(jnp.int32, (1, LANES), 1)
        cos_half = jnp.where(lane < MLA_ROPE, cos128, 1.0)
        sin_half = jnp.where(lane < MLA_ROPE, sin128, 0.0)

    def da_qk(v, gain, scale):
        y = v * _group_rsqrt(v, g64_ref[...], DA_DK) * (gain * scale)
        if rope:
            y = _rope(y, cos512, sin512)
        return y.astype(BF16)

    dq_ref[0, rs, :] = da_qk(p[:, 0:n_dq], dqg_ref[...], DA_DK ** -0.5 * LOG2E)
    dk_ref[0, rs, :] = da_qk(p[:, n_dq:2 * n_dq], dkg_ref[...], 1.0)
    ones = jnp.ones((p.shape[0], LANES), BF16)
    dv = p[:, 2 * n_dq:3 * n_dq].astype(BF16)
    dv_ref[0, rs, :] = jnp.concatenate(
        [blk for hh in range(DA_HEADS) for blk in (dv[:, hh * DA_DV:(hh + 1) * DA_DV], ones)], axis=1)

    c0 = 3 * n_dq
    cq = p[:, c0:c0 + MLA_Q_LORA]
    cq = cq * _group_rsqrt(cq, g256_ref[...], MLA_Q_LORA) * qag_ref[...]
    q = _dot(cq.astype(BF16), wuq_ref[...])
    q = q * _group_rsqrt(q, g256_ref[...], MLA_NOPE + MLA_ROPE) * (
        mqg_ref[...] * ((MLA_NOPE + MLA_ROPE) ** -0.5 * LOG2E))

    c1 = c0 + MLA_Q_LORA
    ckv = p[:, c1:c1 + MLA_KV_LORA]
    ckv = ckv * _group_rsqrt(ckv, g128_ref[...], MLA_KV_LORA) * kvag_ref[...]
    kv = _dot(ckv.astype(BF16), wukv_ref[...])

    c2 = c1 + MLA_KV_LORA
    kr = p[:, c2:c2 + LANES]
    kr = kr * _group_rsqrt(kr, g128_ref[...], MLA_ROPE) * krg_ref[...]
    if rope:
        kr = _rope(kr, cos_half, sin_half)
    kr = kr.astype(BF16)

    mq, mk, mv = [], [], []
    for hh in range(MLA_HEADS):
        b0 = hh * 2 * LANES
        qn = q[:, b0:b0 + LANES]
        qr = q[:, b0 + LANES:b0 + 2 * LANES]
        if rope:
            qr = _rope(qr, cos_half, sin_half)
        mq += [qn.astype(BF16), qr.astype(BF16)]
        kn = kv[:, b0:b0 + LANES]
        kn = kn * _group_rsqrt(kn, g128_ref[...], MLA_NOPE) * mkg_ref[...]
        mk += [kn.astype(BF16), kr]
        mv += [kv[:, b0 + LANES:b0 + 2 * LANES].astype(BF16), ones]
    mq_ref[0, rs, :] = jnp.concatenate(mq, axis=1)
    mk_ref[0, rs, :] = jnp.concatenate(mk, axis=1)
    mv_ref[0, rs, :] = jnp.concatenate(mv, axis=1)


def _const_spec(a):
    nd = a.ndim
    return pl.BlockSpec(a.shape, lambda b, i: (0,) * nd)


def _mod_spec(a):
    if a.shape[0] == 1:
        return pl.BlockSpec((1, 1, a.shape[2]), lambda b, i: (0, 0, 0))
    return pl.BlockSpec((1, 1, a.shape[2]), lambda b, i: (b, 0, 0))


def _proj_even(x, shift, scale, gain, consts, cos, sin, rope, tm):
    bn, t, d = x.shape
    tok = lambda w: pl.BlockSpec((1, tm, w), lambda b, i: (b, i, 0))
    rope_spec = pl.BlockSpec((tm, LANES), lambda b, i: (i, 0))
    widths = (512, 512, 1024, 1024, 1024, 1024)
    return pl.pallas_call(
        functools.partial(_proj_even_kernel, rope),
        grid=(bn, t // tm),
        in_specs=[tok(d), _mod_spec(shift), _mod_spec(scale), _const_spec(gain)]
                 + [_const_spec(c) for c in consts] + [rope_spec, rope_spec],
        out_specs=[tok(w) for w in widths],
        out_shape=[jax.ShapeDtypeStruct((bn, t, w), BF16) for w in widths],
        compiler_params=pltpu.CompilerParams(
            dimension_semantics=("arbitrary", "arbitrary"), vmem_limit_bytes=VMEM_LIMIT),
        name="proj_even",
    )(x, shift, scale, gain, *consts, cos, sin)


def _softmax_parts(s_list):
    m = functools.reduce(jnp.maximum, [jnp.max(s, axis=-1, keepdims=True) for s in s_list])
    e_list = [jnp.exp2(s - m) for s in s_list]
    l = functools.reduce(jnp.add, [jnp.sum(e, axis=-1, keepdims=True) for e in e_list])
    return e_list, l


def _diff_attn_kernel(nseg, nsub, lam_init, *refs):
    q_ref = refs[0]
    k_refs = refs[1:1 + nseg]
    v_refs = refs[1 + nseg:1 + 2 * nseg]
    lqk_ref, og_ref, o_ref = refs[1 + 2 * nseg:]
    lqk = lqk_ref[...]
    lam = (jnp.exp(jnp.sum(lqk[0:1] * lqk[1:2], axis=-1, keepdims=True))
           - jnp.exp(jnp.sum(lqk[2:3] * lqk[3:4], axis=-1, keepdims=True)) + lam_init)
    lane = lax.broadcasted_iota(jnp.int32, (1, LANES), 1)
    rows = q_ref.shape[1] // nsub
    for t in range(nsub):
        sl = slice(t * rows, (t + 1) * rows)
        q = q_ref[0, sl, :]
        zero = jnp.zeros_like(q)
        q1 = jnp.where(lane < DA_DK, q, zero)
        q2 = jnp.where(lane >= DA_DK, q, zero)
        e1, l1 = _softmax_parts([_dot_nt(q1, k[0]) for k in k_refs])
        e2, l2 = _softmax_parts([_dot_nt(q2, k[0]) for k in k_refs])
        r1 = 1.0 / l1
        r2 = lam / l2
        o = functools.reduce(jnp.add, [_dot((a * r1 - b * r2).astype(BF16), v[0])
                                       for a, b, v in zip(e1, e2, v_refs)])
        o = o * lax.rsqrt(jnp.mean(o * o, axis=-1, keepdims=True) + EPS) * (og_ref[...] * (1.0 - lam_init))
        o_ref[0, sl, :] = o.astype(BF16)


def _diff_attn(q, ks, vs, lqk, out_g, lam_init, tq, nsub):
    bn, t, _ = q.shape
    nseg = len(ks)
    tq = min(tq, t)
    qspec = pl.BlockSpec((1, tq, LANES), lambda b, h, i: (b, i, h))
    kspec = lambda a: pl.BlockSpec((1, a.shape[1], LANES), lambda b, h, i: (b, 0, h))
    vspec = lambda a: pl.BlockSpec((1, a.shape[1], LANES), lambda b, h, i: (b, 0, 2 * h))
    cspec = lambda a: pl.BlockSpec(a.shape, lambda b, h, i: (0, 0))
    return pl.pallas_call(
        functools.partial(_diff_attn_kernel, nseg, nsub, lam_init),
        grid=(bn, DA_HEADS, t // tq),
        in_specs=[qspec] + [kspec(k) for k in ks] + [vspec(v) for v in vs] + [cspec(lqk), cspec(out_g)],
        out_specs=qspec,
        out_shape=jax.ShapeDtypeStruct((bn, t, DA_HEADS * DA_DV), BF16),
        compiler_params=pltpu.CompilerParams(
            dimension_semantics=("arbitrary",) * 3, vmem_limit_bytes=VMEM_LIMIT),
        name="diff_attn",
    )(q, *ks, *vs, lqk, out_g)


def _mla_attn_kernel(nseg, nsub, *refs):
    q_ref = refs[0]
    k_refs = refs[1:1 + nseg]
    v_refs = refs[1 + nseg:1 + 2 * nseg]
    o_ref = refs[1 + 2 * nseg]
    rows = q_ref.shape[1] // nsub
    for t in range(nsub):
        sl = slice(t * rows, (t + 1) * rows)
        q = q_ref[0, sl, :]
        e, l = _softmax_parts([_dot_nt(q, k[0]) for k in k_refs])
        o = functools.reduce(jnp.add, [_dot(a.astype(BF16), v[0]) for a, v in zip(e, v_refs)])
        o_ref[0, sl, :] = (o * (1.0 / l)).astype(BF16)


def _mla_attn(q, ks, vs, tq, nsub):
    bn, t, _ = q.shape
    nseg = len(ks)
    tq = min(tq, t)
    hw = 2 * LANES
    qspec = pl.BlockSpec((1, tq, hw), lambda b, h, i: (b, i, h))
    kspec = lambda a: pl.BlockSpec((1, a.shape[1], hw), lambda b, h, i: (b, 0, h))
    vspec = lambda a: pl.BlockSpec((1, a.shape[1], LANES), lambda b, h, i: (b, 0, 2 * h))
    return pl.pallas_call(
        functools.partial(_mla_attn_kernel, nseg, nsub),
        grid=(bn, MLA_HEADS, t // tq),
        in_specs=[qspec] + [kspec(k) for k in ks] + [vspec(v) for v in vs],
        out_specs=pl.BlockSpec((1, tq, LANES), lambda b, h, i: (b, i, h)),
        out_shape=jax.ShapeDtypeStruct((bn, t, MLA_HEADS * MLA_V), BF16),
        compiler_params=pltpu.CompilerParams(
            dimension_semantics=("arbitrary",) * 3, vmem_limit_bytes=VMEM_LIMIT),
        name="mla_attn",
    )(q, *ks, *vs)


def _next_step_maps(nb, nh, ns):
    last = nb * nh * ns - 1

    def nxt(b, h, i):
        u = jnp.minimum((b * nh + h) * ns + i + 1, last)
        return u // (nh * ns), (u // ns) % nh, u % ns

    return nxt


def _is_first_step():
    return (pl.program_id(0) == 0) & (pl.program_id(1) == 0) & (pl.program_id(2) == 0)


def _lane_bcast(col, rows):
    return jnp.broadcast_to(col, (rows, LANES))


def _attn_pipe_kernel(diff, lam_init, *refs):
    n_in = 10 if diff else 8
    q_ref, qn_ref, kc_ref, kx_ref, kcn_ref, kxn_ref, vc_ref, vx_ref = refs[:8]
    o_ref = refs[n_in]
    s_refs = refs[n_in + 1:n_in + 1 + PIPE_TILES]
    m_refs = refs[n_in + 1 + PIPE_TILES:]
    rows = s_refs[0].shape[0]
    tq = rows // 2 if diff else rows
    nc = kc_ref.shape[1]
    nk = s_refs[0].shape[1]
    lane = lax.broadcasted_iota(jnp.int32, (1, LANES), 1)

    def scores(q, kc, kx, s_ref, m_ref):
        if diff:
            zero = jnp.zeros_like(q)
            q = jnp.concatenate([jnp.where(lane < DA_DK, q, zero), jnp.where(lane < DA_DK, zero, q)], axis=0)
        sc = _dot_nt(q, kc[0])
        sx = _dot_nt(q, kx[0])
        s_ref[:, :nc] = sc
        s_ref[:, nc:] = sx
        m = jnp.maximum(jnp.max(sc, axis=-1, keepdims=True), jnp.max(sx, axis=-1, keepdims=True))
        m_ref[...] = _lane_bcast(m, rows)

    def attend(s_ref, m_ref, out_rows):
        m = jnp.concatenate([m_ref[...]] * 2, axis=1)
        acc = None
        for j in range(nk // MXU_DIM):
            k0 = j * MXU_DIM
            p = jnp.exp2(s_ref[:, k0:k0 + MXU_DIM] - m).astype(BF16)
            v = vc_ref[0, k0:k0 + MXU_DIM, :] if k0 < nc else vx_ref[0, k0 - nc:k0 - nc + MXU_DIM, :]
            part = _dot(p, v)
            acc = part if acc is None else acc + part
        o = acc[:, :LANES] * (1.0 / acc[:, LANES:])
        if diff:
            lqk_ref, og_ref = refs[8:10]
            lqk = lqk_ref[...]
            lam = (jnp.exp(jnp.sum(lqk[0:1] * lqk[1:2], axis=-1, keepdims=True))
                   - jnp.exp(jnp.sum(lqk[2:3] * lqk[3:4], axis=-1, keepdims=True)) + lam_init)
            o = o[:tq] - lam * o[tq:]
            o = o * lax.rsqrt(jnp.mean(o * o, axis=-1, keepdims=True) + EPS) * (og_ref[...] * (1.0 - lam_init))
        o_ref[0, out_rows, :] = o.astype(BF16)

    def tile_rows(t):
        return slice(t * tq, (t + 1) * tq)

    @pl.when(_is_first_step())
    def _():
        for t in range(2):
            scores(q_ref[0, tile_rows(t), :], kc_ref, kx_ref, s_refs[t], m_refs[t])

    for t in range(PIPE_TILES):
        attend(s_refs[t], m_refs[t], tile_rows(t))
        ahead = t + 2
        if ahead < PIPE_TILES:
            scores(q_ref[0, tile_rows(ahead), :], kc_ref, kx_ref, s_refs[ahead], m_refs[ahead])
        else:
            ahead -= PIPE_TILES
            scores(qn_ref[0, tile_rows(ahead), :], kcn_ref, kxn_ref, s_refs[ahead], m_refs[ahead])


def _attn_pipe(q, kc, kx, vc, vx, tq, diff_params=None):
    diff = diff_params is not None
    bn, t, _ = q.shape
    nh = DA_HEADS if diff else MLA_HEADS
    hw = LANES if diff else 2 * LANES
    step_rows = PIPE_TILES * tq
    ns = t // step_rows
    assert t % step_rows == 0
    rows = 2 * tq if diff else tq
    nkeys = kc.shape[1] + kx.shape[1]
    nxt = _next_step_maps(bn, nh, ns)

    def q_next(b, h, i):
        b2, h2, i2 = nxt(b, h, i)
        return b2, i2, h2

    def k_next(b, h, i):
        b2, h2, _ = nxt(b, h, i)
        return b2, 0, h2

    cur = lambda b, h, i: (b, 0, h)
    cspec = lambda a: pl.BlockSpec(a.shape, lambda b, h, i: (0, 0))
    extra = list(diff_params[:2]) if diff else []
    lam_init = diff_params[2] if diff else 0.0
    return pl.pallas_call(
        functools.partial(_attn_pipe_kernel, diff, lam_init),
        grid=(bn, nh, ns),
        in_specs=[pl.BlockSpec((1, step_rows, hw), lambda b, h, i: (b, i, h)),
                  pl.BlockSpec((1, step_rows, hw), q_next),
                  pl.BlockSpec((1, kc.shape[1], hw), cur),
                  pl.BlockSpec((1, kx.shape[1], hw), cur),
                  pl.BlockSpec((1, kc.shape[1], hw), k_next),
                  pl.BlockSpec((1, kx.shape[1], hw), k_next),
                  pl.BlockSpec((1, vc.shape[1], 2 * LANES), cur),
                  pl.BlockSpec((1, vx.shape[1], 2 * LANES), cur)]
                 + [cspec(a) for a in extra],
        out_specs=pl.BlockSpec((1, step_rows, LANES), lambda b, h, i: (b, i, h)),
        out_shape=jax.ShapeDtypeStruct((bn, t, nh * LANES), BF16),
        scratch_shapes=[pltpu.VMEM((rows, nkeys), F32)] * PIPE_TILES + [pltpu.VMEM((rows, LANES), F32)] * PIPE_TILES,
        compiler_params=pltpu.CompilerParams(
            dimension_semantics=("arbitrary",) * 3, vmem_limit_bytes=VMEM_LIMIT),
        name="diff_pipe" if diff else "mla_pipe",
    )(q, q, kc, kx, kc, kx, vc, vx, *extra)


def _post_kernel(ny, *refs):
    x_ref = refs[0]
    y_refs = refs[1:1 + ny]
    ga_ref, shf_ref, scf_ref, gf_ref, g_ref = refs[1 + ny:6 + ny]
    wo_refs = refs[6 + ny:6 + 2 * ny]
    win_ref, wout_ref, o_ref = refs[6 + 2 * ny:]
    for t in range(x_ref.shape[1] // POST_ROWS):
        rs = slice(t * POST_ROWS, (t + 1) * POST_ROWS)
        attn = functools.reduce(jnp.add, [_dot(y[0, rs, :], w[...]) for y, w in zip(y_refs, wo_refs)])
        x1 = x_ref[0, rs, :] + ga_ref[0] * attn
        h = _modulated_norm(x1, g_ref[...], shf_ref[0], scf_ref[0]).astype(BF16)
        u = _dot(h, win_ref[...])
        gate = u[:, :D_FF]
        act = (gate * jax.nn.sigmoid(gate) * u[:, D_FF:]).astype(BF16)
        o_ref[0, rs, :] = x1 + gf_ref[0] * _dot(act, wout_ref[...])


def _post(x, ys, gate_a, shift_f, scale_f, gate_f, gain, w_os, w_in, w_out, tm):
    bn, t, d = x.shape
    tok = lambda w: pl.BlockSpec((1, tm, w), lambda b, i: (b, i, 0))
    wspec = lambda a: pl.BlockSpec(a.shape, lambda b, i: (0, 0), pipeline_mode=pl.Buffered(1))
    return pl.pallas_call(
        functools.partial(_post_kernel, len(ys)),
        grid=(bn, t // tm),
        in_specs=[tok(d)] + [tok(y.shape[2]) for y in ys]
                 + [_mod_spec(gate_a), _mod_spec(shift_f), _mod_spec(scale_f), _mod_spec(gate_f), _const_spec(gain)]
                 + [wspec(w) for w in w_os] + [wspec(w_in), wspec(w_out)],
        out_specs=tok(d),
        out_shape=jax.ShapeDtypeStruct((bn, t, d), F32),
        compiler_params=pltpu.CompilerParams(
            dimension_semantics=("arbitrary", "arbitrary"), vmem_limit_bytes=VMEM_LIMIT),
        name="post",
    )(x, *ys, gate_a, shift_f, scale_f, gate_f, gain, *w_os, w_in, w_out)


def _proj_odd_kernel(*refs):
    for t in range(refs[0].shape[1] // PROJ_ROWS):
        _proj_odd_tile(slice(t * PROJ_ROWS, (t + 1) * PROJ_ROWS), *refs)


def _proj_odd_tile(rs, x_ref, sh_ref, sc_ref, g_ref, w_ref, g64_ref, qg_ref, kg_ref, q_ref, k_ref, v_ref):
    h = _modulated_norm(x_ref[0, rs, :], g_ref[...], sh_ref[0], sc_ref[0]).astype(BF16)
    p = _dot(h, w_ref[...])
    w = NA_HEADS * NA_DH

    def qk(v, gain, scale):
        return (v * _group_rsqrt(v, g64_ref[...], NA_DH) * (gain * scale)).astype(BF16)

    q_ref[0, rs, :] = qk(p[:, :w], qg_ref[...], NA_DH ** -0.5 * LOG2E)
    k_ref[0, rs, :] = qk(p[:, w:2 * w], kg_ref[...], 1.0)
    v = p[:, 2 * w:].astype(BF16)
    ones = jnp.ones((p.shape[0], LANES), BF16)
    v_ref[0, rs, :] = jnp.concatenate(
        [blk for j in range(w // LANES) for blk in (v[:, j * LANES:(j + 1) * LANES], ones)], axis=1)


def _proj_odd(x, shift, scale, gain, consts, tm):
    bn, t, d = x.shape
    tok = lambda w: pl.BlockSpec((1, tm, w), lambda b, i: (b, i, 0))
    w = NA_HEADS * NA_DH
    widths = (w, w, 2 * w)
    return pl.pallas_call(
        _proj_odd_kernel,
        grid=(bn, t // tm),
        in_specs=[tok(d), _mod_spec(shift), _mod_spec(scale), _const_spec(gain)] + [_const_spec(c) for c in consts],
        out_specs=[tok(n) for n in widths],
        out_shape=[jax.ShapeDtypeStruct((bn, t, n), BF16) for n in widths],
        compiler_params=pltpu.CompilerParams(
            dimension_semantics=("arbitrary", "arbitrary"), vmem_limit_bytes=VMEM_LIMIT),
        name="proj_odd",
    )(x, shift, scale, gain, *consts)


def _na_window_start(g):
    return min(max(NA_GROUP * g - NA_KH // 2, 0), NA_ROWS - NA_WIN_ROWS)


def _na_bias_tile(g, bias_r, lo):
    ws = _na_window_start(g)
    neg = jnp.full((GRID_W, LANES), NEG_INF, F32)
    row_tiles = []
    for head in range(2):
        for a in range(NA_GROUP):
            r = NA_GROUP * g + a
            rs = min(max(r - NA_KH // 2, 0), NA_ROWS - NA_KH)
            blocks = []
            for jp in range(NA_WIN_ROWS // 2):
                kr0 = ws + 2 * jp
                ok0 = rs <= kr0 < rs + NA_KH
                ok1 = rs <= kr0 + 1 < rs + NA_KH
                if not (ok0 or ok1):
                    blocks.append(neg)
                    continue
                blk = bias_r[0, head, kr0 - r + NA_KH - 1]
                if not ok0:
                    blk = jnp.where(lo, neg, blk)
                if not ok1:
                    blk = jnp.where(lo, blk, neg)
                blocks.append(blk)
            row_tiles.append(jnp.concatenate(blocks, axis=1))
    return jnp.concatenate(row_tiles, axis=0)


def _na_kernel(q_ref, k_ref, v_ref, kc_ref, vc_ref, bias_ref, qn_ref, kn_ref, kcn_ref, biasn_ref, o_ref,
               *scratch):
    s_refs = scratch[:PIPE_TILES]
    m_refs = scratch[PIPE_TILES:]
    lane = lax.broadcasted_iota(jnp.int32, (1, LANES), 1)
    lo = lane < NA_DH
    gq = NA_GROUP * GRID_W
    nloc = NA_WIN_ROWS * GRID_W
    nk = s_refs[0].shape[1]

    def win(g):
        k0 = _na_window_start(g) * GRID_W
        return slice(k0, k0 + nloc)

    def scores(g, q_r, k_r, kc_r, bias_r, s_ref, m_ref):
        q = q_r[0, g * gq:(g + 1) * gq, :]
        zero = jnp.zeros_like(q)
        qq = jnp.concatenate([jnp.where(lo, q, zero), jnp.where(lo, zero, q)], axis=0)
        s_loc = _dot_nt(qq, k_r[0, win(g), :]) + _na_bias_tile(g, bias_r, lo)
        s_ctx = _dot_nt(qq, kc_r[0])
        s_ref[:, :nloc] = s_loc
        s_ref[:, nloc:] = s_ctx
        m = jnp.maximum(jnp.max(s_loc, axis=-1, keepdims=True), jnp.max(s_ctx, axis=-1, keepdims=True))
        m_ref[...] = _lane_bcast(m, 2 * gq)

    def attend(g, s_ref, m_ref):
        m = jnp.concatenate([m_ref[...]] * 2, axis=1)
        k0 = _na_window_start(g) * GRID_W
        acc = None
        for j in range(nk // MXU_DIM):
            c0 = j * MXU_DIM
            p = jnp.exp2(s_ref[:, c0:c0 + MXU_DIM] - m).astype(BF16)
            v = v_ref[0, k0 + c0:k0 + c0 + MXU_DIM, :] if c0 < nloc else vc_ref[0, c0 - nloc:c0 - nloc + MXU_DIM, :]
            part = _dot(p, v)
            acc = part if acc is None else acc + part
        o = acc[:, :LANES] * (1.0 / acc[:, LANES:])
        o_ref[0, g * gq:(g + 1) * gq, :] = jnp.where(lo, o[:gq], o[gq:]).astype(BF16)

    @pl.when((pl.program_id(0) == 0) & (pl.program_id(1) == 0))
    def _():
        for g in range(2):
            scores(g, q_ref, k_ref, kc_ref, bias_ref, s_refs[g], m_refs[g])

    for g in range(NA_GROUPS):
        attend(g, s_refs[g % PIPE_TILES], m_refs[g % PIPE_TILES])
        ahead = g + 2
        slot = ahead % PIPE_TILES
        if ahead < NA_GROUPS:
            scores(ahead, q_ref, k_ref, kc_ref, bias_ref, s_refs[slot], m_refs[slot])
        else:
            scores(ahead - NA_GROUPS, qn_ref, kn_ref, kcn_ref, biasn_ref, s_refs[slot], m_refs[slot])


def _na_attn(q, k, v, kc, vc, bias):
    bn, s, w = q.shape
    assert s == NA_ROWS * GRID_W and NA_GROUPS % PIPE_TILES == 0
    pairs = w // LANES
    last = pairs * bn - 1

    def nxt(p, b):
        u = jnp.minimum(p * bn + b + 1, last)
        return u % bn, 0, u // bn

    cur = lambda p, b: (b, 0, p)
    spec = lambda a, width, imap: pl.BlockSpec((1, a.shape[1], width), imap)
    rows = 2 * NA_GROUP * GRID_W
    nk = NA_WIN_ROWS * GRID_W + kc.shape[1]
    return pl.pallas_call(
        _na_kernel,
        grid=(pairs, bn),
        in_specs=[spec(q, LANES, cur), spec(k, LANES, cur), spec(v, 2 * LANES, cur),
                  spec(kc, LANES, cur), spec(vc, 2 * LANES, cur),
                  pl.BlockSpec((1,) + bias.shape[1:], lambda p, b: (p, 0, 0, 0, 0)),
                  spec(q, LANES, nxt), spec(k, LANES, nxt), spec(kc, LANES, nxt),
                  pl.BlockSpec((1,) + bias.shape[1:], lambda p, b: (nxt(p, b)[2], 0, 0, 0, 0))],
        out_specs=spec(q, LANES, cur),
        out_shape=jax.ShapeDtypeStruct((bn, s, w), BF16),
        scratch_shapes=[pltpu.VMEM((rows, nk), F32)] * PIPE_TILES + [pltpu.VMEM((rows, LANES), F32)] * PIPE_TILES,
        compiler_params=pltpu.CompilerParams(
            dimension_semantics=("arbitrary", "arbitrary"), vmem_limit_bytes=VMEM_LIMIT),
        name="na_attn",
    )(q, k, v, kc, vc, bias, q, k, kc, bias)


def _group_matrix(width, group):
    idx = np.arange(width) // group
    return jnp.asarray((idx[:, None] == idx[None, :]).astype(np.float32), dtype=BF16)


def _rope_tables(s):
    t = np.arange(s)
    row = (t // GRID_W).astype(np.float32)
    col = (t % GRID_W).astype(np.float32)
    half = DA_DK // 2
    inv = jnp.asarray(ROPE_THETA, F32) ** (-jnp.arange(0, half, 2, dtype=F32) / half)
    ar = jnp.asarray(row)[:, None] * inv
    ac = jnp.asarray(col)[:, None] * inv
    cr, sr, cc, sc = jnp.cos(ar), jnp.sin(ar), jnp.cos(ac), jnp.sin(ac)
    cos64 = jnp.concatenate([cr, cr, cc, cc], axis=1)
    sin64 = jnp.concatenate([-sr, sr, -sc, sc], axis=1)
    return jnp.concatenate([cos64, cos64], axis=1), jnp.concatenate([sin64, sin64], axis=1)


def _na_bias_table(rpb):
    cols = np.arange(GRID_W)
    col_start = np.clip(cols - NA_KW // 2, 0, GRID_W - NA_KW)
    col_ok = (cols[None, :] >= col_start[:, None]) & (cols[None, :] < col_start[:, None] + NA_KW)
    dc_idx = np.clip(cols[None, :] - cols[:, None], -(NA_KW - 1), NA_KW - 1) + NA_KW - 1
    by_col = jnp.where(jnp.asarray(col_ok)[None, None], rpb[:, :, dc_idx] * LOG2E, NEG_INF)
    pairs = jnp.concatenate([by_col[:, :-1], by_col[:, 1:]], axis=-1)
    return pairs.reshape(NA_HEADS // 2, 2, 2 * NA_KH - 2, GRID_W, 2 * GRID_W)


def _tile_lanes(g, n):
    return jnp.tile(g, n).reshape(1, -1)


def kernel(x, c, ctx, c_ctx, mod_w, mod_b, norm_mix_g, norm_ffn_g, w_out, ffn_w_in, ffn_w_out, ev_w_in, da_q_g, da_k_g, da_lq1, da_lk1, da_lq2, da_lk2, da_out_g, mla_q_a_g, mla_w_uq, mla_kv_a_g, mla_w_ukv, mla_q_g, mla_k_g, mla_kr_g, od_w_in, na_q_g, na_k_g, na_rpb):
    bsz, seq, d = x.shape
    tm = 256
    tm_x = 2 * PROJ_ROWS
    tq = 256

    cond = jnp.concatenate([c, c_ctx[None], jnp.zeros((7, d), F32)], axis=0)
    mods = _adaln(cond, mod_w, mod_b)

    def mod_vectors(l, lo, hi):
        return [mods[l, lo:hi, j * d:(j + 1) * d].reshape(hi - lo, 1, d) for j in range(6)]

    g64 = _group_matrix(MXU_DIM, 64)
    cos, sin = _rope_tables(seq)

    l = 0
    w_in = jnp.concatenate([ev_w_in[0], jnp.zeros((d, 64), F32)], axis=1).astype(BF16)
    wuq = mla_w_uq[0].reshape(MLA_Q_LORA, MLA_HEADS, MLA_NOPE + MLA_ROPE)
    wuq = jnp.pad(wuq, ((0, 0), (0, 0), (0, 64))).reshape(MLA_Q_LORA, MLA_HEADS * 256).astype(BF16)
    mqg = jnp.tile(jnp.pad(mla_q_g[0], (0, 64)), MLA_HEADS).reshape(1, -1)
    consts = [
        w_in, g64, _group_matrix(LANES, LANES), _group_matrix(MXU_DIM, MXU_DIM),
        _tile_lanes(da_q_g[0], 8), _tile_lanes(da_k_g[0], 8),
        mla_q_a_g[0].reshape(1, -1), wuq, mla_kv_a_g[0].reshape(1, -1), mla_w_ukv[0].astype(BF16),
        mqg, mla_k_g[0].reshape(1, -1), jnp.pad(mla_kr_g[0], (0, 64)).reshape(1, -1),
    ]
    lqk = jnp.stack([da_lq1[0], da_lk1[0], da_lq2[0], da_lk2[0]])
    lam_init = 0.8 - 0.6 * math.exp(-0.3 * l)
    out_g = da_out_g[0].reshape(1, -1)
    w_o = w_out[l].astype(BF16)
    w_os = [w_o[:DA_HEADS * DA_DV], w_o[DA_HEADS * DA_DV:]]
    f_in = ffn_w_in[l].astype(BF16)
    f_out = ffn_w_out[l].astype(BF16)
    gm = norm_mix_g[l].reshape(1, d)
    gf = norm_ffn_g[l].reshape(1, d)

    sh_a, sc_a, g_a, sh_f, sc_f, g_f = mod_vectors(l, 0, bsz)
    csh_a, csc_a, cg_a, csh_f, csc_f, cg_f = mod_vectors(l, bsz, bsz + 1)

    px = _proj_even(x, sh_a, sc_a, gm, consts, cos, sin, True, tm_x)
    pc = _proj_even(ctx, csh_a, csc_a, gm, consts, cos[:ctx.shape[1]], sin[:ctx.shape[1]], False, tm)
    dq_x, dk_x, dv_x, mq_x, mk_x, mv_x = px
    dq_c, dk_c, dv_c, mq_c, mk_c, mv_c = pc

    da_x = _attn_pipe(dq_x, dk_c, dk_x, dv_c, dv_x, tq, (lqk, out_g, lam_init))
    mla_x = _attn_pipe(mq_x, mk_c, mk_x, mv_c, mv_x, tq)
    da_c = _diff_attn(dq_c, [dk_c], [dv_c], lqk, out_g, lam_init, tq, 1)
    mla_c = _mla_attn(mq_c, [mk_c], [mv_c], tq, 1)

    x = _post(x, [da_x, mla_x], g_a, sh_f, sc_f, g_f, gf, w_os, f_in, f_out, 2 * POST_ROWS)
    ctx = _post(ctx, [da_c, mla_c], cg_a, csh_f, csc_f, cg_f, gf, w_os, f_in, f_out, tm)

    l = 1
    sh_a, sc_a, g_a, sh_f, sc_f, g_f = mod_vectors(l, 0, bsz)
    csh_a, csc_a = mod_vectors(l, bsz, bsz + 1)[:2]
    gm = norm_mix_g[l].reshape(1, d)
    gf = norm_ffn_g[l].reshape(1, d)
    consts = [od_w_in[0].astype(BF16), g64, _tile_lanes(na_q_g[0], NA_HEADS), _tile_lanes(na_k_g[0], NA_HEADS)]
    q_x, k_x, v_x = _proj_odd(x, sh_a, sc_a, gm, consts, tm_x)
    _, k_c, v_c = _proj_odd(ctx, csh_a, csc_a, gm, consts, tm)
    y = _na_attn(q_x, k_x, v_x, k_c, v_c, _na_bias_table(na_rpb[0]))
    x = _post(x, [y], g_a, sh_f, sc_f, g_f, gf, [w_out[l].astype(BF16)],
              ffn_w_in[l].astype(BF16), ffn_w_out[l].astype(BF16), 2 * POST_ROWS)
    return x
```

```python
import functools
import math

import jax
import jax.numpy as jnp
import numpy as np
from jax import lax
from jax.experimental import pallas as pl
from jax.experimental.pallas import tpu as pltpu

D_MODEL = 1024
DEPTH = 2
GRID_W = 64
DA_HEADS = 4
DA_DK = 64
DA_DV = 2 * DA_DK
MLA_HEADS = 4
MLA_NOPE = 128
MLA_ROPE = 64
MLA_V = 128
MLA_Q_LORA = 256
MLA_KV_LORA = 128
NA_HEADS = 16
NA_DH = 64
NA_KH = 8
NA_KW = 16
D_FF = -(-8 * D_MODEL // (3 * 256)) * 256
ROPE_THETA = 10000.0
EPS = 1e-6
NEG_INF = -1e30
LOG2E = math.log2(math.e)
SEQ = 2048
NA_ROWS = SEQ // GRID_W
NA_GROUP = 4
NA_GROUPS = NA_ROWS // NA_GROUP
NA_WIN_ROWS = NA_GROUP + NA_KH

LANES = 128
MXU_DIM = 256
VMEM_LIMIT = 56 * 1024 * 1024

PROJ_ROWS = 256
POST_ROWS = 256
PIPE_TILES = 4
ATTN_STEP_TILES = 8

BF16 = jnp.bfloat16
F32 = jnp.float32


def _dot(a, b):
    return jnp.dot(a, b, preferred_element_type=F32)


def _dot_nt(a, b):
    return lax.dot_general(a, b, (((1,), (1,)), ((), ())), preferred_element_type=F32)


def _group_rsqrt(x, gmat, size):
    x2 = (x * x).astype(BF16)
    w = gmat.shape[0]
    n = x2.shape[1] // w
    sums = [_dot(x2[:, i * w:(i + 1) * w], gmat) for i in range(n)]
    ss = sums[0] if n == 1 else jnp.concatenate(sums, axis=1)
    return lax.rsqrt(ss * (1.0 / size) + EPS)


def _swap16(x):
    n = x.shape[1]
    lane = lax.broadcasted_iota(jnp.int32, (1, n), 1)
    up = pltpu.roll(x, n - 16, 1)
    down = pltpu.roll(x, 16, 1)
    return jnp.where((lane % 32) < 16, up, down)


def _rope(x, cos, sin_signed):
    return x * cos + _swap16(x) * sin_signed


def _modulated_norm(x, g, shift, scale):
    y = x * lax.rsqrt(jnp.mean(x * x, axis=-1, keepdims=True) + EPS)
    return (y * g) * (1.0 + scale) + shift


def _adaln_kernel(cond_ref, w_ref, b_ref, o_ref):
    c = cond_ref[...]
    a = (c * jax.nn.sigmoid(c)).astype(BF16)
    res = _dot(a, w_ref[0].astype(BF16)) + b_ref[0]
    for r in range(res.shape[0]):
        o_ref[0, r] = res[r:r + 1, :]


def _adaln(cond, mod_w, mod_b):
    rows = cond.shape[0]
    n = mod_w.shape[2]
    tn = 1536
    return pl.pallas_call(
        _adaln_kernel,
        grid=(DEPTH, n // tn),
        in_specs=[
            pl.BlockSpec((rows, D_MODEL), lambda l, j: (0, 0)),
            pl.BlockSpec((1, D_MODEL, tn), lambda l, j: (l, 0, j)),
            pl.BlockSpec((1, 1, tn), lambda l, j: (l, 0, j)),
        ],
        out_specs=pl.BlockSpec((1, rows, 1, tn), lambda l, j: (l, 0, 0, j)),
        out_shape=jax.ShapeDtypeStruct((DEPTH, rows, 1, n), F32),
        compiler_params=pltpu.CompilerParams(
            dimension_semantics=("arbitrary", "arbitrary"), vmem_limit_bytes=VMEM_LIMIT),
        name="adaln",
    )(cond, mod_w, mod_b.reshape(DEPTH, 1, n))


def _proj_even_kernel(rope, *refs):
    for t in range(refs[0].shape[1] // PROJ_ROWS):
        _proj_even_tile(rope, slice(t * PROJ_ROWS, (t + 1) * PROJ_ROWS), *refs)


def _proj_even_tile(rope, rs, x_ref, sh_ref, sc_ref, g_ref, w_ref, g64_ref, g128_ref, g256_ref,
                    dqg_ref, dkg_ref, qag_ref, wuq_ref, kvag_ref, wukv_ref, mqg_ref, mkg_ref,
                    krg_ref, cos_ref, sin_ref,
                    dq_ref, dk_ref, dv_ref, mq_ref, mk_ref, mv_ref):
    h = _modulated_norm(x_ref[0, rs, :], g_ref[...], sh_ref[0, 0], sc_ref[0, 0]).astype(BF16)
    p = _dot(h, w_ref[...])
    n_dq = DA_HEADS * 2 * DA_DK
    if rope:
        cos128 = cos_ref[rs, :]
        sin128 = sin_ref[rs, :]
        cos512 = jnp.concatenate([cos128] * 4, axis=1)
        sin512 = jnp.concatenate([sin128] * 4, axis=1)
        lane = lax.broadcasted_iota(jnp.int32, (1, LANES), 1)
        cos_half = jnp.where(lane < MLA_ROPE, cos128, 1.0)
        sin_half = jnp.where(lane < MLA_ROPE, sin128, 0.0)

    def da_qk(v, gain, scale):
        y = v * _group_rsqrt(v, g64_ref[...], DA_DK) * (gain * scale)
        if rope:
            y = _rope(y, cos512, sin512)
        return y.astype(BF16)

    dq_ref[0, rs, :] = da_qk(p[:, 0:n_dq], dqg_ref[...], DA_DK ** -0.5 * LOG2E)
    dk_ref[0, rs, :] = da_qk(p[:, n_dq:2 * n_dq], dkg_ref[...], 1.0)
    ones = jnp.ones((p.shape[0], LANES), BF16)
    dv = p[:, 2 * n_dq:3 * n_dq].astype(BF16)
    dv_ref[0, rs, :] = jnp.concatenate(
        [blk for hh in range(DA_HEADS) for blk in (dv[:, hh * DA_DV:(hh + 1) * DA_DV], ones)], axis=1)

    c0 = 3 * n_dq
    cq = p[:, c0:c0 + MLA_Q_LORA]
    cq = cq * _group_rsqrt(cq, g256_ref[...], MLA_Q_LORA) * qag_ref[...]
    q = _dot(cq.astype(BF16), wuq_ref[...])
    q = q * _group_rsqrt(q, g256_ref[...], MLA_NOPE + MLA_ROPE) * (
        mqg_ref[...] * ((MLA_NOPE + MLA_ROPE) ** -0.5 * LOG2E))

    c1 = c0 + MLA_Q_LORA
    ckv = p[:, c1:c1 + MLA_KV_LORA]
    ckv = ckv * _group_rsqrt(ckv, g128_ref[...], MLA_KV_LORA) * kvag_ref[...]
    kv = _dot(ckv.astype(BF16), wukv_ref[...])

    c2 = c1 + MLA_KV_LORA
    kr = p[:, c2:c2 + LANES]
    kr = kr * _group_rsqrt(kr, g128_ref[...], MLA_ROPE) * krg_ref[...]
    if rope:
        kr = _rope(kr, cos_half, sin_half)
    kr = kr.astype(BF16)

    mq, mk, mv = [], [], []
    for hh in range(MLA_HEADS):
        b0 = hh * 2 * LANES
        qn = q[:, b0:b0 + LANES]
        qr = q[:, b0 + LANES:b0 + 2 * LANES]
        if rope:
            qr = _rope(qr, cos_half, sin_half)
        mq += [qn.astype(BF16), qr.astype(BF16)]
        kn = kv[:, b0:b0 + LANES]
        kn = kn * _group_rsqrt(kn, g128_ref[...], MLA_NOPE) * mkg_ref[...]
        mk += [kn.astype(BF16), kr]
        mv += [kv[:, b0 + LANES:b0 + 2 * LANES].astype(BF16), ones]
    mq_ref[0, rs, :] = jnp.concatenate(mq, axis=1)
    mk_ref[0, rs, :] = jnp.concatenate(mk, axis=1)
    mv_ref[0, rs, :] = jnp.concatenate(mv, axis=1)


def _const_spec(a):
    nd = a.ndim
    return pl.BlockSpec(a.shape, lambda b, i: (0,) * nd)


def _mod_spec(vec):
    l, row, j = vec
    return pl.BlockSpec((1, 1, 1, D_MODEL), lambda b, i: (l, b if row is None else row, 0, j))


def _proj_even(x, mods, shift, scale, gain, consts, cos, sin, rope, tm):
    bn, t, d = x.shape
    tok = lambda w: pl.BlockSpec((1, tm, w), lambda b, i: (b, i, 0))
    rope_spec = pl.BlockSpec((tm, LANES), lambda b, i: (i, 0))
    widths = (512, 512, 1024, 1024, 1024, 1024)
    return pl.pallas_call(
        functools.partial(_proj_even_kernel, rope),
        grid=(bn, t // tm),
        in_specs=[tok(d), _mod_spec(shift), _mod_spec(scale), _const_spec(gain)]
                 + [_const_spec(c) for c in consts] + [rope_spec, rope_spec],
        out_specs=[tok(w) for w in widths],
        out_shape=[jax.ShapeDtypeStruct((bn, t, w), BF16) for w in widths],
        compiler_params=pltpu.CompilerParams(
            dimension_semantics=("arbitrary", "arbitrary"), vmem_limit_bytes=VMEM_LIMIT),
        name="proj_even",
    )(x, mods, mods, gain, *consts, cos, sin)


def _softmax_parts(s_list):
    m = functools.reduce(jnp.maximum, [jnp.max(s, axis=-1, keepdims=True) for s in s_list])
    e_list = [jnp.exp2(s - m) for s in s_list]
    l = functools.reduce(jnp.add, [jnp.sum(e, axis=-1, keepdims=True) for e in e_list])
    return e_list, l


def _diff_attn_kernel(nseg, nsub, lam_init, *refs):
    q_ref = refs[0]
    k_refs = refs[1:1 + nseg]
    v_refs = refs[1 + nseg:1 + 2 * nseg]
    lqk_ref, og_ref, o_ref = refs[1 + 2 * nseg:]
    lqk = lqk_ref[...]
    lam = (jnp.exp(jnp.sum(lqk[0:1] * lqk[1:2], axis=-1, keepdims=True))
           - jnp.exp(jnp.sum(lqk[2:3] * lqk[3:4], axis=-1, keepdims=True)) + lam_init)
    lane = lax.broadcasted_iota(jnp.int32, (1, LANES), 1)
    rows = q_ref.shape[1] // nsub
    for t in range(nsub):
        sl = slice(t * rows, (t + 1) * rows)
        q = q_ref[0, sl, :]
        zero = jnp.zeros_like(q)
        q1 = jnp.where(lane < DA_DK, q, zero)
        q2 = jnp.where(lane >= DA_DK, q, zero)
        e1, l1 = _softmax_parts([_dot_nt(q1, k[0]) for k in k_refs])
        e2, l2 = _softmax_parts([_dot_nt(q2, k[0]) for k in k_refs])
        r1 = 1.0 / l1
        r2 = lam / l2
        o = functools.reduce(jnp.add, [_dot((a * r1 - b * r2).astype(BF16), v[0])
                                       for a, b, v in zip(e1, e2, v_refs)])
        o = o * lax.rsqrt(jnp.mean(o * o, axis=-1, keepdims=True) + EPS) * (og_ref[...] * (1.0 - lam_init))
        o_ref[0, sl, :] = o.astype(BF16)


def _diff_attn(q, ks, vs, lqk, out_g, lam_init, tq, nsub):
    bn, t, _ = q.shape
    nseg = len(ks)
    tq = min(tq, t)
    qspec = pl.BlockSpec((1, tq, LANES), lambda b, h, i: (b, i, h))
    kspec = lambda a: pl.BlockSpec((1, a.shape[1], LANES), lambda b, h, i: (b, 0, h))
    vspec = lambda a: pl.BlockSpec((1, a.shape[1], LANES), lambda b, h, i: (b, 0, 2 * h))
    cspec = lambda a: pl.BlockSpec(a.shape, lambda b, h, i: (0, 0))
    return pl.pallas_call(
        functools.partial(_diff_attn_kernel, nseg, nsub, lam_init),
        grid=(bn, DA_HEADS, t // tq),
        in_specs=[qspec] + [kspec(k) for k in ks] + [vspec(v) for v in vs] + [cspec(lqk), cspec(out_g)],
        out_specs=qspec,
        out_shape=jax.ShapeDtypeStruct((bn, t, DA_HEADS * DA_DV), BF16),
        compiler_params=pltpu.CompilerParams(
            dimension_semantics=("arbitrary",) * 3, vmem_limit_bytes=VMEM_LIMIT),
        name="diff_attn",
    )(q, *ks, *vs, lqk, out_g)


def _mla_attn_kernel(nseg, nsub, *refs):
    q_ref = refs[0]
    k_refs = refs[1:1 + nseg]
    v_refs = refs[1 + nseg:1 + 2 * nseg]
    o_ref = refs[1 + 2 * nseg]
    rows = q_ref.shape[1] // nsub
    for t in range(nsub):
        sl = slice(t * rows, (t + 1) * rows)
        q = q_ref[0, sl, :]
        e, l = _softmax_parts([_dot_nt(q, k[0]) for k in k_refs])
        o = functools.reduce(jnp.add, [_dot(a.astype(BF16), v[0]) for a, v in zip(e, v_refs)])
        o_ref[0, sl, :] = (o * (1.0 / l)).astype(BF16)


def _mla_attn(q, ks, vs, tq, nsub):
    bn, t, _ = q.shape
    nseg = len(ks)
    tq = min(tq, t)
    hw = 2 * LANES
    qspec = pl.BlockSpec((1, tq, hw), lambda b, h, i: (b, i, h))
    kspec = lambda a: pl.BlockSpec((1, a.shape[1], hw), lambda b, h, i: (b, 0, h))
    vspec = lambda a: pl.BlockSpec((1, a.shape[1], LANES), lambda b, h, i: (b, 0, 2 * h))
    return pl.pallas_call(
        functools.partial(_mla_attn_kernel, nseg, nsub),
        grid=(bn, MLA_HEADS, t // tq),
        in_specs=[qspec] + [kspec(k) for k in ks] + [vspec(v) for v in vs],
        out_specs=pl.BlockSpec((1, tq, LANES), lambda b, h, i: (b, i, h)),
        out_shape=jax.ShapeDtypeStruct((bn, t, MLA_HEADS * MLA_V), BF16),
        compiler_params=pltpu.CompilerParams(
            dimension_semantics=("arbitrary",) * 3, vmem_limit_bytes=VMEM_LIMIT),
        name="mla_attn",
    )(q, *ks, *vs)


def _next_step_maps(nb, nh, ns):
    last = nb * nh * ns - 1

    def nxt(b, h, i):
        u = jnp.minimum((b * nh + h) * ns + i + 1, last)
        return u // (nh * ns), (u // ns) % nh, u % ns

    return nxt


def _is_first_step():
    return (pl.program_id(0) == 0) & (pl.program_id(1) == 0) & (pl.program_id(2) == 0)


def _lane_bcast(col, rows):
    return jnp.broadcast_to(col, (rows, LANES))


def _attn_pipe_kernel(diff, lam_init, *refs):
    n_in = 10 if diff else 8
    q_ref, qn_ref, kc_ref, kx_ref, kcn_ref, kxn_ref, vc_ref, vx_ref = refs[:8]
    o_ref = refs[n_in]
    s_refs = refs[n_in + 1:n_in + 1 + PIPE_TILES]
    m_refs = refs[n_in + 1 + PIPE_TILES:]
    rows = s_refs[0].shape[0]
    tq = rows // 2 if diff else rows
    nc = kc_ref.shape[1]
    nk = s_refs[0].shape[1]
    lane = lax.broadcasted_iota(jnp.int32, (1, LANES), 1)

    def scores(q, kc, kx, s_ref, m_ref):
        if diff:
            zero = jnp.zeros_like(q)
            q = jnp.concatenate([jnp.where(lane < DA_DK, q, zero), jnp.where(lane < DA_DK, zero, q)], axis=0)
        sc = _dot_nt(q, kc[0])
        sx = _dot_nt(q, kx[0])
        s_ref[:, :nc] = sc
        s_ref[:, nc:] = sx
        m = jnp.maximum(jnp.max(sc, axis=-1, keepdims=True), jnp.max(sx, axis=-1, keepdims=True))
        m_ref[...] = _lane_bcast(m, rows)

    def attend(s_ref, m_ref, out_rows):
        m = jnp.concatenate([m_ref[...]] * 2, axis=1)
        acc = None
        for j in range(nk // MXU_DIM):
            k0 = j * MXU_DIM
            p = jnp.exp2(s_ref[:, k0:k0 + MXU_DIM] - m).astype(BF16)
            v = vc_ref[0, k0:k0 + MXU_DIM, :] if k0 < nc else vx_ref[0, k0 - nc:k0 - nc + MXU_DIM, :]
            part = _dot(p, v)
            acc = part if acc is None else acc + part
        o = acc[:, :LANES] * (1.0 / acc[:, LANES:])
        if diff:
            lqk_ref, og_ref = refs[8:10]
            lqk = lqk_ref[...]
            lam = (jnp.exp(jnp.sum(lqk[0:1] * lqk[1:2], axis=-1, keepdims=True))
                   - jnp.exp(jnp.sum(lqk[2:3] * lqk[3:4], axis=-1, keepdims=True)) + lam_init)
            o = o[:tq] - lam * o[tq:]
            o = o * lax.rsqrt(jnp.mean(o * o, axis=-1, keepdims=True) + EPS) * (og_ref[...] * (1.0 - lam_init))
        o_ref[0, out_rows, :] = o.astype(BF16)

    def tile_rows(t):
        return slice(t * tq, (t + 1) * tq)

    @pl.when(_is_first_step())
    def _():
        for t in range(2):
            scores(q_ref[0, tile_rows(t), :], kc_ref, kx_ref, s_refs[t], m_refs[t])

    n_tiles = q_ref.shape[1] // tq
    for t in range(n_tiles):
        attend(s_refs[t % PIPE_TILES], m_refs[t % PIPE_TILES], tile_rows(t))
        ahead = t + 2
        slot = ahead % PIPE_TILES
        if ahead < n_tiles:
            scores(q_ref[0, tile_rows(ahead), :], kc_ref, kx_ref, s_refs[slot], m_refs[slot])
        else:
            scores(qn_ref[0, tile_rows(ahead - n_tiles), :], kcn_ref, kxn_ref, s_refs[slot], m_refs[slot])


def _attn_pipe(q, kc, kx, vc, vx, tq, diff_params=None):
    diff = diff_params is not None
    bn, t, _ = q.shape
    nh = DA_HEADS if diff else MLA_HEADS
    hw = LANES if diff else 2 * LANES
    step_rows = ATTN_STEP_TILES * tq
    ns = t // step_rows
    assert t % step_rows == 0 and ATTN_STEP_TILES % PIPE_TILES == 0
    rows = 2 * tq if diff else tq
    nkeys = kc.shape[1] + kx.shape[1]
    nxt = _next_step_maps(bn, nh, ns)

    def q_next(b, h, i):
        b2, h2, i2 = nxt(b, h, i)
        return b2, i2, h2

    def k_next(b, h, i):
        b2, h2, _ = nxt(b, h, i)
        return b2, 0, h2

    cur = lambda b, h, i: (b, 0, h)
    cspec = lambda a: pl.BlockSpec(a.shape, lambda b, h, i: (0, 0))
    extra = list(diff_params[:2]) if diff else []
    lam_init = diff_params[2] if diff else 0.0
    return pl.pallas_call(
        functools.partial(_attn_pipe_kernel, diff, lam_init),
        grid=(bn, nh, ns),
        in_specs=[pl.BlockSpec((1, step_rows, hw), lambda b, h, i: (b, i, h)),
                  pl.BlockSpec((1, step_rows, hw), q_next),
                  pl.BlockSpec((1, kc.shape[1], hw), cur),
                  pl.BlockSpec((1, kx.shape[1], hw), cur),
                  pl.BlockSpec((1, kc.shape[1], hw), k_next),
                  pl.BlockSpec((1, kx.shape[1], hw), k_next),
                  pl.BlockSpec((1, vc.shape[1], 2 * LANES), cur),
                  pl.BlockSpec((1, vx.shape[1], 2 * LANES), cur)]
                 + [cspec(a) for a in extra],
        out_specs=pl.BlockSpec((1, step_rows, LANES), lambda b, h, i: (b, i, h)),
        out_shape=jax.ShapeDtypeStruct((bn, t, nh * LANES), BF16),
        scratch_shapes=[pltpu.VMEM((rows, nkeys), F32)] * PIPE_TILES + [pltpu.VMEM((rows, LANES), F32)] * PIPE_TILES,
        compiler_params=pltpu.CompilerParams(
            dimension_semantics=("arbitrary",) * 3, vmem_limit_bytes=VMEM_LIMIT),
        name="diff_pipe" if diff else "mla_pipe",
    )(q, q, kc, kx, kc, kx, vc, vx, *extra)


def _post_kernel(ny, *refs):
    x_ref = refs[0]
    y_refs = refs[1:1 + ny]
    ga_ref, shf_ref, scf_ref, gf_ref, g_ref = refs[1 + ny:6 + ny]
    wo_refs = refs[6 + ny:6 + 2 * ny]
    win_ref, wout_ref, o_ref = refs[6 + 2 * ny:]
    for t in range(x_ref.shape[1] // POST_ROWS):
        rs = slice(t * POST_ROWS, (t + 1) * POST_ROWS)
        attn = functools.reduce(jnp.add, [_dot(y[0, rs, :], w[...]) for y, w in zip(y_refs, wo_refs)])
        x1 = x_ref[0, rs, :] + ga_ref[0, 0] * attn
        h = _modulated_norm(x1, g_ref[...], shf_ref[0, 0], scf_ref[0, 0]).astype(BF16)
        u = _dot(h, win_ref[...])
        gate = u[:, :D_FF]
        act = (gate * jax.nn.sigmoid(gate) * u[:, D_FF:]).astype(BF16)
        o_ref[0, rs, :] = x1 + gf_ref[0, 0] * _dot(act, wout_ref[...])


def _post(x, ys, mods, gate_a, shift_f, scale_f, gate_f, gain, w_os, w_in, w_out, tm):
    bn, t, d = x.shape
    tok = lambda w: pl.BlockSpec((1, tm, w), lambda b, i: (b, i, 0))
    wspec = lambda a: pl.BlockSpec(a.shape, lambda b, i: (0, 0), pipeline_mode=pl.Buffered(1))
    return pl.pallas_call(
        functools.partial(_post_kernel, len(ys)),
        grid=(bn, t // tm),
        in_specs=[tok(d)] + [tok(y.shape[2]) for y in ys]
                 + [_mod_spec(gate_a), _mod_spec(shift_f), _mod_spec(scale_f), _mod_spec(gate_f), _const_spec(gain)]
                 + [wspec(w) for w in w_os] + [wspec(w_in), wspec(w_out)],
        out_specs=tok(d),
        out_shape=jax.ShapeDtypeStruct((bn, t, d), F32),
        compiler_params=pltpu.CompilerParams(
            dimension_semantics=("arbitrary", "arbitrary"), vmem_limit_bytes=VMEM_LIMIT),
        name="post",
    )(x, *ys, mods, mods, mods, mods, gain, *w_os, w_in, w_out)


def _proj_odd_kernel(*refs):
    for t in range(refs[0].shape[1] // PROJ_ROWS):
        _proj_odd_tile(slice(t * PROJ_ROWS, (t + 1) * PROJ_ROWS), *refs)


def _proj_odd_tile(rs, x_ref, sh_ref, sc_ref, g_ref, w_ref, g64_ref, qg_ref, kg_ref, q_ref, k_ref, v_ref):
    h = _modulated_norm(x_ref[0, rs, :], g_ref[...], sh_ref[0, 0], sc_ref[0, 0]).astype(BF16)
    p = _dot(h, w_ref[...])
    w = NA_HEADS * NA_DH

    def qk(v, gain, scale):
        return (v * _group_rsqrt(v, g64_ref[...], NA_DH) * (gain * scale)).astype(BF16)

    q_ref[0, rs, :] = qk(p[:, :w], qg_ref[...], NA_DH ** -0.5 * LOG2E)
    k_ref[0, rs, :] = qk(p[:, w:2 * w], kg_ref[...], 1.0)
    v = p[:, 2 * w:].astype(BF16)
    ones = jnp.ones((p.shape[0], LANES), BF16)
    v_ref[0, rs, :] = jnp.concatenate(
        [blk for j in range(w // LANES) for blk in (v[:, j * LANES:(j + 1) * LANES], ones)], axis=1)


def _proj_odd(x, mods, shift, scale, gain, consts, tm):
    bn, t, d = x.shape
    tok = lambda w: pl.BlockSpec((1, tm, w), lambda b, i: (b, i, 0))
    w = NA_HEADS * NA_DH
    widths = (w, w, 2 * w)
    return pl.pallas_call(
        _proj_odd_kernel,
        grid=(bn, t // tm),
        in_specs=[tok(d), _mod_spec(shift), _mod_spec(scale), _const_spec(gain)] + [_const_spec(c) for c in consts],
        out_specs=[tok(n) for n in widths],
        out_shape=[jax.ShapeDtypeStruct((bn, t, n), BF16) for n in widths],
        compiler_params=pltpu.CompilerParams(
            dimension_semantics=("arbitrary", "arbitrary"), vmem_limit_bytes=VMEM_LIMIT),
        name="proj_odd",
    )(x, mods, mods, gain, *consts)


def _na_window_start(g):
    return min(max(NA_GROUP * g - NA_KH // 2, 0), NA_ROWS - NA_WIN_ROWS)


def _na_bias_tile(g, bias_r, lo):
    ws = _na_window_start(g)
    neg = jnp.full((GRID_W, LANES), NEG_INF, F32)
    row_tiles = []
    for head in range(2):
        for a in range(NA_GROUP):
            r = NA_GROUP * g + a
            rs = min(max(r - NA_KH // 2, 0), NA_ROWS - NA_KH)
            blocks = []
            for jp in range(NA_WIN_ROWS // 2):
                kr0 = ws + 2 * jp
                ok0 = rs <= kr0 < rs + NA_KH
                ok1 = rs <= kr0 + 1 < rs + NA_KH
                if not (ok0 or ok1):
                    blocks.append(neg)
                    continue
                blk = bias_r[0, head, kr0 - r + NA_KH - 1]
                if not ok0:
                    blk = jnp.where(lo, neg, blk)
                if not ok1:
                    blk = jnp.where(lo, blk, neg)
                blocks.append(blk)
            row_tiles.append(jnp.concatenate(blocks, axis=1))
    return jnp.concatenate(row_tiles, axis=0)


def _na_kernel(q_ref, k_ref, v_ref, kc_ref, vc_ref, bias_ref, qn_ref, kn_ref, kcn_ref, biasn_ref, o_ref,
               *scratch):
    s_refs = scratch[:PIPE_TILES]
    m_refs = scratch[PIPE_TILES:]
    lane = lax.broadcasted_iota(jnp.int32, (1, LANES), 1)
    lo = lane < NA_DH
    gq = NA_GROUP * GRID_W
    nloc = NA_WIN_ROWS * GRID_W
    nk = s_refs[0].shape[1]

    def win(g):
        k0 = _na_window_start(g) * GRID_W
        return slice(k0, k0 + nloc)

    def scores(g, q_r, k_r, kc_r, bias_r, s_ref, m_ref):
        q = q_r[0, g * gq:(g + 1) * gq, :]
        zero = jnp.zeros_like(q)
        qq = jnp.concatenate([jnp.where(lo, q, zero), jnp.where(lo, zero, q)], axis=0)
        s_loc = _dot_nt(qq, k_r[0, win(g), :]) + _na_bias_tile(g, bias_r, lo)
        s_ctx = _dot_nt(qq, kc_r[0])
        s_ref[:, :nloc] = s_loc
        s_ref[:, nloc:] = s_ctx
        m = jnp.maximum(jnp.max(s_loc, axis=-1, keepdims=True), jnp.max(s_ctx, axis=-1, keepdims=True))
        m_ref[...] = _lane_bcast(m, 2 * gq)

    def attend(g, s_ref, m_ref):
        m = jnp.concatenate([m_ref[...]] * 2, axis=1)
        k0 = _na_window_start(g) * GRID_W
        acc = None
        for j in range(nk // MXU_DIM):
            c0 = j * MXU_DIM
            p = jnp.exp2(s_ref[:, c0:c0 + MXU_DIM] - m).astype(BF16)
            v = v_ref[0, k0 + c0:k0 + c0 + MXU_DIM, :] if c0 < nloc else vc_ref[0, c0 - nloc:c0 - nloc + MXU_DIM, :]
            part = _dot(p, v)
            acc = part if acc is None else acc + part
        o = acc[:, :LANES] * (1.0 / acc[:, LANES:])
        o_ref[0, g * gq:(g + 1) * gq, :] = jnp.where(lo, o[:gq], o[gq:]).astype(BF16)

    @pl.when((pl.program_id(0) == 0) & (pl.program_id(1) == 0))
    def _():
        for g in range(2):
            scores(g, q_ref, k_ref, kc_ref, bias_ref, s_refs[g], m_refs[g])

    for g in range(NA_GROUPS):
        attend(g, s_refs[g % PIPE_TILES], m_refs[g % PIPE_TILES])
        ahead = g + 2
        slot = ahead % PIPE_TILES
        if ahead < NA_GROUPS:
            scores(ahead, q_ref, k_ref, kc_ref, bias_ref, s_refs[slot], m_refs[slot])
        else:
            scores(ahead - NA_GROUPS, qn_ref, kn_ref, kcn_ref, biasn_ref, s_refs[slot], m_refs[slot])


def _na_attn(q, k, v, kc, vc, bias):
    bn, s, w = q.shape
    assert s == NA_ROWS * GRID_W and NA_GROUPS % PIPE_TILES == 0
    pairs = w // LANES
    last = pairs * bn - 1

    def nxt(p, b):
        u = jnp.minimum(p * bn + b + 1, last)
        return u % bn, 0, u // bn

    cur = lambda p, b: (b, 0, p)
    spec = lambda a, width, imap: pl.BlockSpec((1, a.shape[1], width), imap)
    rows = 2 * NA_GROUP * GRID_W
    nk = NA_WIN_ROWS * GRID_W + kc.shape[1]
    return pl.pallas_call(
        _na_kernel,
        grid=(pairs, bn),
        in_specs=[spec(q, LANES, cur), spec(k, LANES, cur), spec(v, 2 * LANES, cur),
                  spec(kc, LANES, cur), spec(vc, 2 * LANES, cur),
                  pl.BlockSpec((1,) + bias.shape[1:], lambda p, b: (p, 0, 0, 0, 0)),
                  spec(q, LANES, nxt), spec(k, LANES, nxt), spec(kc, LANES, nxt),
                  pl.BlockSpec((1,) + bias.shape[1:], lambda p, b: (nxt(p, b)[2], 0, 0, 0, 0))],
        out_specs=spec(q, LANES, cur),
        out_shape=jax.ShapeDtypeStruct((bn, s, w), BF16),
        scratch_shapes=[pltpu.VMEM((rows, nk), F32)] * PIPE_TILES + [pltpu.VMEM((rows, LANES), F32)] * PIPE_TILES,
        compiler_params=pltpu.CompilerParams(
            dimension_semantics=("arbitrary", "arbitrary"), vmem_limit_bytes=VMEM_LIMIT),
        name="na_attn",
    )(q, k, v, kc, vc, bias, q, k, kc, bias)


def _group_matrix(width, group):
    idx = np.arange(width) // group
    return jnp.asarray((idx[:, None] == idx[None, :]).astype(np.float32), dtype=BF16)


def _rope_tables(s):
    t = np.arange(s)
    row = (t // GRID_W).astype(np.float32)
    col = (t % GRID_W).astype(np.float32)
    half = DA_DK // 2
    inv = (np.float32(ROPE_THETA) ** (-np.arange(0, half, 2, dtype=np.float32) / np.float32(half))).astype(np.float32)
    ar = row[:, None] * inv
    ac = col[:, None] * inv
    cr, sr, cc, sc = np.cos(ar), np.sin(ar), np.cos(ac), np.sin(ac)
    cos64 = np.concatenate([cr, cr, cc, cc], axis=1)
    sin64 = np.concatenate([-sr, sr, -sc, sc], axis=1)
    return (jnp.asarray(np.concatenate([cos64, cos64], axis=1), F32),
            jnp.asarray(np.concatenate([sin64, sin64], axis=1), F32))


def _na_bias_table(rpb):
    cols = np.arange(GRID_W)
    col_start = np.clip(cols - NA_KW // 2, 0, GRID_W - NA_KW)
    col_ok = (cols[None, :] >= col_start[:, None]) & (cols[None, :] < col_start[:, None] + NA_KW)
    dc_idx = np.clip(cols[None, :] - cols[:, None], -(NA_KW - 1), NA_KW - 1) + NA_KW - 1
    by_col = jnp.where(jnp.asarray(col_ok)[None, None], rpb[:, :, dc_idx] * LOG2E, NEG_INF)
    pairs = jnp.concatenate([by_col[:, :-1], by_col[:, 1:]], axis=-1)
    return pairs.reshape(NA_HEADS // 2, 2, 2 * NA_KH - 2, GRID_W, 2 * GRID_W)


def _tile_lanes(g, n):
    return jnp.tile(g, n).reshape(1, -1)


def kernel(x, c, ctx, c_ctx, mod_w, mod_b, norm_mix_g, norm_ffn_g, w_out, ffn_w_in, ffn_w_out, ev_w_in, da_q_g, da_k_g, da_lq1, da_lk1, da_lq2, da_lk2, da_out_g, mla_q_a_g, mla_w_uq, mla_kv_a_g, mla_w_ukv, mla_q_g, mla_k_g, mla_kr_g, od_w_in, na_q_g, na_k_g, na_rpb):
    bsz, seq, d = x.shape
    tm = 256
    tm_x = 2 * PROJ_ROWS
    tq = 256

    cond = jnp.concatenate([c, c_ctx[None], jnp.zeros((7, d), F32)], axis=0)
    mods = _adaln(cond, mod_w, mod_b)

    def mod_vectors(l, row):
        return [(l, row, j) for j in range(6)]

    g64 = _group_matrix(MXU_DIM, 64)
    cos, sin = _rope_tables(seq)

    l = 0
    w_in = jnp.concatenate([ev_w_in[0], jnp.zeros((d, 64), F32)], axis=1).astype(BF16)
    wuq = mla_w_uq[0].reshape(MLA_Q_LORA, MLA_HEADS, MLA_NOPE + MLA_ROPE)
    wuq = jnp.pad(wuq, ((0, 0), (0, 0), (0, 64))).reshape(MLA_Q_LORA, MLA_HEADS * 256).astype(BF16)
    mqg = jnp.tile(jnp.pad(mla_q_g[0], (0, 64)), MLA_HEADS).reshape(1, -1)
    consts = [
        w_in, g64, _group_matrix(LANES, LANES), _group_matrix(MXU_DIM, MXU_DIM),
        _tile_lanes(da_q_g[0], 8), _tile_lanes(da_k_g[0], 8),
        mla_q_a_g[0].reshape(1, -1), wuq, mla_kv_a_g[0].reshape(1, -1), mla_w_ukv[0].astype(BF16),
        mqg, mla_k_g[0].reshape(1, -1), jnp.pad(mla_kr_g[0], (0, 64)).reshape(1, -1),
    ]
    lqk = jnp.stack([da_lq1[0], da_lk1[0], da_lq2[0], da_lk2[0]])
    lam_init = 0.8 - 0.6 * math.exp(-0.3 * l)
    out_g = da_out_g[0].reshape(1, -1)
    w_o = w_out[l].astype(BF16)
    w_os = [w_o[:DA_HEADS * DA_DV], w_o[DA_HEADS * DA_DV:]]
    f_in = ffn_w_in[l].astype(BF16)
    f_out = ffn_w_out[l].astype(BF16)
    gm = norm_mix_g[l].reshape(1, d)
    gf = norm_ffn_g[l].reshape(1, d)

    sh_a, sc_a, g_a, sh_f, sc_f, g_f = mod_vectors(l, None)
    csh_a, csc_a, cg_a, csh_f, csc_f, cg_f = mod_vectors(l, bsz)

    px = _proj_even(x, mods, sh_a, sc_a, gm, consts, cos, sin, True, tm_x)
    pc = _proj_even(ctx, mods, csh_a, csc_a, gm, consts, cos[:ctx.shape[1]], sin[:ctx.shape[1]], False, tm)
    dq_x, dk_x, dv_x, mq_x, mk_x, mv_x = px
    dq_c, dk_c, dv_c, mq_c, mk_c, mv_c = pc

    da_x = _attn_pipe(dq_x, dk_c, dk_x, dv_c, dv_x, tq, (lqk, out_g, lam_init))
    mla_x = _attn_pipe(mq_x, mk_c, mk_x, mv_c, mv_x, tq)
    da_c = _diff_attn(dq_c, [dk_c], [dv_c], lqk, out_g, lam_init, tq, 1)
    mla_c = _mla_attn(mq_c, [mk_c], [mv_c], tq, 1)

    x = _post(x, [da_x, mla_x], mods, g_a, sh_f, sc_f, g_f, gf, w_os, f_in, f_out, 2 * POST_ROWS)
    ctx = _post(ctx, [da_c, mla_c], mods, cg_a, csh_f, csc_f, cg_f, gf, w_os, f_in, f_out, tm)

    l = 1
    sh_a, sc_a, g_a, sh_f, sc_f, g_f = mod_vectors(l, None)
    csh_a, csc_a = mod_vectors(l, bsz)[:2]
    gm = norm_mix_g[l].reshape(1, d)
    gf = norm_ffn_g[l].reshape(1, d)
    consts = [od_w_in[0].astype(BF16), g64, _tile_lanes(na_q_g[0], NA_HEADS), _tile_lanes(na_k_g[0], NA_HEADS)]
    q_x, k_x, v_x = _proj_odd(x, mods, sh_a, sc_a, gm, consts, tm_x)
    _, k_c, v_c = _proj_odd(ctx, mods, csh_a, csc_a, gm, consts, tm)
    y = _na_attn(q_x, k_x, v_x, k_c, v_c, _na_bias_table(na_rpb[0]))
    x = _post(x, [y], mods, g_a, sh_f, sc_f, g_f, gf, [w_out[l].astype(BF16)],
              ffn_w_in[l].astype(BF16), ffn_w_out[l].astype(BF16), 2 * POST_ROWS)
    return x
```

```python
import functools
import math

import jax
import jax.numpy as jnp
import numpy as np
from jax import lax
from jax.experimental import pallas as pl
from jax.experimental.pallas import tpu as pltpu

D_MODEL = 1024
DEPTH = 2
GRID_W = 64
DA_HEADS = 4
DA_DK = 64
DA_DV = 2 * DA_DK
MLA_HEADS = 4
MLA_NOPE = 128
MLA_ROPE = 64
MLA_V = 128
MLA_Q_LORA = 256
MLA_KV_LORA = 128
NA_HEADS = 16
NA_DH = 64
NA_KH = 8
NA_KW = 16
D_FF = -(-8 * D_MODEL // (3 * 256)) * 256
ROPE_THETA = 10000.0
EPS = 1e-6
NEG_INF = -1e30
LOG2E = math.log2(math.e)
SEQ = 2048
NA_ROWS = SEQ // GRID_W
NA_GROUP = 4
NA_GROUPS = NA_ROWS // NA_GROUP
NA_WIN_ROWS = NA_GROUP + NA_KH

LANES = 128
MXU_DIM = 256
VMEM_LIMIT = 56 * 1024 * 1024

PROJ_ROWS = 256
POST_STEP_TILES = 4
POST_ROWS = 256
PIPE_TILES = 4
ATTN_STEP_TILES = 8

BF16 = jnp.bfloat16
F32 = jnp.float32


def _dot(a, b):
    return jnp.dot(a, b, preferred_element_type=F32)


def _dot_nt(a, b):
    return lax.dot_general(a, b, (((1,), (1,)), ((), ())), preferred_element_type=F32)


def _group_rsqrt(x, gmat, size):
    x2 = (x * x).astype(BF16)
    w = gmat.shape[0]
    n = x2.shape[1] // w
    sums = [_dot(x2[:, i * w:(i + 1) * w], gmat) for i in range(n)]
    ss = sums[0] if n == 1 else jnp.concatenate(sums, axis=1)
    return lax.rsqrt(ss * (1.0 / size) + EPS)


def _swap16(x):
    n = x.shape[1]
    lane = lax.broadcasted_iota(jnp.int32, (1, n), 1)
    up = pltpu.roll(x, n - 16, 1)
    down = pltpu.roll(x, 16, 1)
    return jnp.where((lane % 32) < 16, up, down)


def _rope(x, cos, sin_signed):
    return x * cos + _swap16(x) * sin_signed


def _modulated_norm(x, g, shift, scale):
    y = x * lax.rsqrt(jnp.mean(x * x, axis=-1, keepdims=True) + EPS)
    return (y * g) * (1.0 + scale) + shift


def _adaln_kernel(cond_ref, w_ref, b_ref, o_ref):
    c = cond_ref[...]
    a = (c * jax.nn.sigmoid(c)).astype(BF16)
    res = _dot(a, w_ref[0].astype(BF16)) + b_ref[0]
    for r in range(res.shape[0]):
        o_ref[0, r] = res[r:r + 1, :]


def _adaln(cond, mod_w, mod_b):
    rows = cond.shape[0]
    n = mod_w.shape[2]
    tn = 1536
    return pl.pallas_call(
        _adaln_kernel,
        grid=(DEPTH, n // tn),
        in_specs=[
            pl.BlockSpec((rows, D_MODEL), lambda l, j: (0, 0)),
            pl.BlockSpec((1, D_MODEL, tn), lambda l, j: (l, 0, j)),
            pl.BlockSpec((1, 1, tn), lambda l, j: (l, 0, j)),
        ],
        out_specs=pl.BlockSpec((1, rows, 1, tn), lambda l, j: (l, 0, 0, j)),
        out_shape=jax.ShapeDtypeStruct((DEPTH, rows, 1, n), F32),
        compiler_params=pltpu.CompilerParams(
            dimension_semantics=("arbitrary", "arbitrary"), vmem_limit_bytes=VMEM_LIMIT),
        name="adaln",
    )(cond, mod_w, mod_b.reshape(DEPTH, 1, n))


def _proj_even_kernel(rope, *refs):
    for t in range(refs[0].shape[1] // PROJ_ROWS):
        _proj_even_tile(rope, slice(t * PROJ_ROWS, (t + 1) * PROJ_ROWS), *refs)


def _proj_even_tile(rope, rs, x_ref, sh_ref, sc_ref, g_ref, w_ref, g64_ref, g128_ref, g256_ref,
                    dqg_ref, dkg_ref, qag_ref, wuq_ref, kvag_ref, wukv_ref, mqg_ref, mkg_ref,
                    krg_ref, cos_ref, sin_ref,
                    dq_ref, dk_ref, dv_ref, mq_ref, mk_ref, mv_ref):
    h = _modulated_norm(x_ref[0, rs, :], g_ref[...], sh_ref[0, 0], sc_ref[0, 0]).astype(BF16)
    p = _dot(h, w_ref[...])
    n_dq = DA_HEADS * 2 * DA_DK
    if rope:
        cos128 = cos_ref[rs, :]
        sin128 = sin_ref[rs, :]
        cos512 = jnp.concatenate([cos128] * 4, axis=1)
        sin512 = jnp.concatenate([sin128] * 4, axis=1)
        lane = lax.broadcasted_iota(jnp.int32, (1, LANES), 1)
        cos_half = jnp.where(lane < MLA_ROPE, cos128, 1.0)
        sin_half = jnp.where(lane < MLA_ROPE, sin128, 0.0)

    def da_qk(v, gain, scale):
        y = v * _group_rsqrt(v, g64_ref[...], DA_DK) * (gain * scale)
        if rope:
            y = _rope(y, cos512, sin512)
        return y.astype(BF16)

    dq_ref[0, rs, :] = da_qk(p[:, 0:n_dq], dqg_ref[...], DA_DK ** -0.5 * LOG2E)
    dk_ref[0, rs, :] = da_qk(p[:, n_dq:2 * n_dq], dkg_ref[...], 1.0)
    ones = jnp.ones((p.shape[0], LANES), BF16)
    dv = p[:, 2 * n_dq:3 * n_dq].astype(BF16)
    dv_ref[0, rs, :] = jnp.concatenate(
        [blk for hh in range(DA_HEADS) for blk in (dv[:, hh * DA_DV:(hh + 1) * DA_DV], ones)], axis=1)

    c0 = 3 * n_dq
    cq = p[:, c0:c0 + MLA_Q_LORA]
    cq = cq * _group_rsqrt(cq, g256_ref[...], MLA_Q_LORA) * qag_ref[...]
    q = _dot(cq.astype(BF16), wuq_ref[...])
    q = q * _group_rsqrt(q, g256_ref[...], MLA_NOPE + MLA_ROPE) * (
        mqg_ref[...] * ((MLA_NOPE + MLA_ROPE) ** -0.5 * LOG2E))

    c1 = c0 + MLA_Q_LORA
    ckv = p[:, c1:c1 + MLA_KV_LORA]
    ckv = ckv * _group_rsqrt(ckv, g128_ref[...], MLA_KV_LORA) * kvag_ref[...]
    kv = _dot(ckv.astype(BF16), wukv_ref[...])

    c2 = c1 + MLA_KV_LORA
    kr = p[:, c2:c2 + LANES]
    kr = kr * _group_rsqrt(kr, g128_ref[...], MLA_ROPE) * krg_ref[...]
    if rope:
        kr = _rope(kr, cos_half, sin_half)
    kr = kr.astype(BF16)

    mq, mk, mv = [], [], []
    for hh in range(MLA_HEADS):
        b0 = hh * 2 * LANES
        qn = q[:, b0:b0 + LANES]
        qr = q[:, b0 + LANES:b0 + 2 * LANES]
        if rope:
            qr = _rope(qr, cos_half, sin_half)
        mq += [qn.astype(BF16), qr.astype(BF16)]
        kn = kv[:, b0:b0 + LANES]
        kn = kn * _group_rsqrt(kn, g128_ref[...], MLA_NOPE) * mkg_ref[...]
        mk += [kn.astype(BF16), kr]
        mv += [kv[:, b0 + LANES:b0 + 2 * LANES].astype(BF16), ones]
    mq_ref[0, rs, :] = jnp.concatenate(mq, axis=1)
    mk_ref[0, rs, :] = jnp.concatenate(mk, axis=1)
    mv_ref[0, rs, :] = jnp.concatenate(mv, axis=1)


def _const_spec(a):
    nd = a.ndim
    return pl.BlockSpec(a.shape, lambda b, i: (0,) * nd)


def _mod_spec(vec):
    l, row, j = vec
    return pl.BlockSpec((1, 1, 1, D_MODEL), lambda b, i: (l, b if row is None else row, 0, j))


def _proj_even(x, mods, shift, scale, gain, consts, cos, sin, rope, tm):
    bn, t, d = x.shape
    tok = lambda w: pl.BlockSpec((1, tm, w), lambda b, i: (b, i, 0))
    rope_spec = pl.BlockSpec((tm, LANES), lambda b, i: (i, 0))
    widths = (512, 512, 1024, 1024, 1024, 1024)
    return pl.pallas_call(
        functools.partial(_proj_even_kernel, rope),
        grid=(bn, t // tm),
        in_specs=[tok(d), _mod_spec(shift), _mod_spec(scale), _const_spec(gain)]
                 + [_const_spec(c) for c in consts] + [rope_spec, rope_spec],
        out_specs=[tok(w) for w in widths],
        out_shape=[jax.ShapeDtypeStruct((bn, t, w), BF16) for w in widths],
        compiler_params=pltpu.CompilerParams(
            dimension_semantics=("arbitrary", "arbitrary"), vmem_limit_bytes=VMEM_LIMIT),
        name="proj_even",
    )(x, mods, mods, gain, *consts, cos, sin)


def _softmax_parts(s_list):
    m = functools.reduce(jnp.maximum, [jnp.max(s, axis=-1, keepdims=True) for s in s_list])
    e_list = [jnp.exp2(s - m) for s in s_list]
    l = functools.reduce(jnp.add, [jnp.sum(e, axis=-1, keepdims=True) for e in e_list])
    return e_list, l


def _ctx_diff_attn_kernel(lam_init, q_ref, k_ref, v_ref, lqk_ref, og_ref, o_ref):
    lqk = lqk_ref[...]
    lam = (jnp.exp(jnp.sum(lqk[0:1] * lqk[1:2], axis=-1, keepdims=True))
           - jnp.exp(jnp.sum(lqk[2:3] * lqk[3:4], axis=-1, keepdims=True)) + lam_init)
    lane = lax.broadcasted_iota(jnp.int32, (1, LANES), 1)
    for h in range(DA_HEADS):
        hs = slice(h * LANES, (h + 1) * LANES)
        q = q_ref[0, :, hs]
        k = k_ref[0, :, hs]
        v = v_ref[0, :, 2 * h * LANES:(2 * h + 1) * LANES]
        zero = jnp.zeros_like(q)
        (e1,), l1 = _softmax_parts([_dot_nt(jnp.where(lane < DA_DK, q, zero), k)])
        (e2,), l2 = _softmax_parts([_dot_nt(jnp.where(lane < DA_DK, zero, q), k)])
        o = _dot((e1 * (1.0 / l1) - e2 * (lam / l2)).astype(BF16), v)
        o = o * lax.rsqrt(jnp.mean(o * o, axis=-1, keepdims=True) + EPS) * (og_ref[...] * (1.0 - lam_init))
        o_ref[0, :, hs] = o.astype(BF16)


def _ctx_diff_attn(q, k, v, lqk, out_g, lam_init):
    bn, n, _ = q.shape
    spec = lambda a: pl.BlockSpec((1, n, a.shape[2]), lambda b: (b, 0, 0))
    cspec = lambda a: pl.BlockSpec(a.shape, lambda b: (0, 0))
    return pl.pallas_call(
        functools.partial(_ctx_diff_attn_kernel, lam_init),
        grid=(bn,),
        in_specs=[spec(q), spec(k), spec(v), cspec(lqk), cspec(out_g)],
        out_specs=spec(q),
        out_shape=jax.ShapeDtypeStruct((bn, n, DA_HEADS * DA_DV), BF16),
        compiler_params=pltpu.CompilerParams(dimension_semantics=("arbitrary",), vmem_limit_bytes=VMEM_LIMIT),
        name="ctx_diff_attn",
    )(q, k, v, lqk, out_g)


def _ctx_mla_attn_kernel(q_ref, k_ref, v_ref, o_ref):
    hw = 2 * LANES
    for h in range(MLA_HEADS):
        q = q_ref[0, :, h * hw:(h + 1) * hw]
        k = k_ref[0, :, h * hw:(h + 1) * hw]
        v = v_ref[0, :, h * hw:h * hw + LANES]
        (e,), l = _softmax_parts([_dot_nt(q, k)])
        o_ref[0, :, h * LANES:(h + 1) * LANES] = (_dot(e.astype(BF16), v) * (1.0 / l)).astype(BF16)


def _ctx_mla_attn(q, k, v):
    bn, n, _ = q.shape
    spec = lambda a: pl.BlockSpec((1, n, a.shape[2]), lambda b: (b, 0, 0))
    return pl.pallas_call(
        _ctx_mla_attn_kernel,
        grid=(bn,),
        in_specs=[spec(q), spec(k), spec(v)],
        out_specs=pl.BlockSpec((1, n, MLA_HEADS * MLA_V), lambda b: (b, 0, 0)),
        out_shape=jax.ShapeDtypeStruct((bn, n, MLA_HEADS * MLA_V), BF16),
        compiler_params=pltpu.CompilerParams(dimension_semantics=("arbitrary",), vmem_limit_bytes=VMEM_LIMIT),
        name="ctx_mla_attn",
    )(q, k, v)


def _next_step_maps(nb, nh, ns):
    last = nb * nh * ns - 1

    def nxt(b, h, i):
        u = jnp.minimum((b * nh + h) * ns + i + 1, last)
        return u // (nh * ns), (u // ns) % nh, u % ns

    return nxt


def _is_first_step():
    return (pl.program_id(0) == 0) & (pl.program_id(1) == 0) & (pl.program_id(2) == 0)


def _lane_bcast(col, rows):
    return jnp.broadcast_to(col, (rows, LANES))


def _attn_pipe_kernel(diff, lam_init, *refs):
    n_in = 10 if diff else 8
    q_ref, qn_ref, kc_ref, kx_ref, kcn_ref, kxn_ref, vc_ref, vx_ref = refs[:8]
    o_ref = refs[n_in]
    s_refs = refs[n_in + 1:n_in + 1 + PIPE_TILES]
    m_refs = refs[n_in + 1 + PIPE_TILES:]
    rows = s_refs[0].shape[0]
    tq = rows // 2 if diff else rows
    nc = kc_ref.shape[1]
    nk = s_refs[0].shape[1]
    lane = lax.broadcasted_iota(jnp.int32, (1, LANES), 1)

    def scores(q, kc, kx, s_ref, m_ref):
        if diff:
            zero = jnp.zeros_like(q)
            q = jnp.concatenate([jnp.where(lane < DA_DK, q, zero), jnp.where(lane < DA_DK, zero, q)], axis=0)
        sc = _dot_nt(q, kc[0])
        sx = _dot_nt(q, kx[0])
        s_ref[:, :nc] = sc
        s_ref[:, nc:] = sx
        m = jnp.maximum(jnp.max(sc, axis=-1, keepdims=True), jnp.max(sx, axis=-1, keepdims=True))
        m_ref[...] = _lane_bcast(m, rows)

    def attend(s_ref, m_ref, out_rows):
        m = jnp.concatenate([m_ref[...]] * 2, axis=1)
        acc = None
        for j in range(nk // MXU_DIM):
            k0 = j * MXU_DIM
            p = jnp.exp2(s_ref[:, k0:k0 + MXU_DIM] - m).astype(BF16)
            v = vc_ref[0, k0:k0 + MXU_DIM, :] if k0 < nc else vx_ref[0, k0 - nc:k0 - nc + MXU_DIM, :]
            part = _dot(p, v)
            acc = part if acc is None else acc + part
        o = acc[:, :LANES] * (1.0 / acc[:, LANES:])
        if diff:
            lqk_ref, og_ref = refs[8:10]
            lqk = lqk_ref[...]
            lam = (jnp.exp(jnp.sum(lqk[0:1] * lqk[1:2], axis=-1, keepdims=True))
                   - jnp.exp(jnp.sum(lqk[2:3] * lqk[3:4], axis=-1, keepdims=True)) + lam_init)
            o = o[:tq] - lam * o[tq:]
            o = o * lax.rsqrt(jnp.mean(o * o, axis=-1, keepdims=True) + EPS) * (og_ref[...] * (1.0 - lam_init))
        o_ref[0, out_rows, :] = o.astype(BF16)

    def tile_rows(t):
        return slice(t * tq, (t + 1) * tq)

    @pl.when(_is_first_step())
    def _():
        for t in range(2):
            scores(q_ref[0, tile_rows(t), :], kc_ref, kx_ref, s_refs[t], m_refs[t])

    n_tiles = q_ref.shape[1] // tq
    for t in range(n_tiles):
        attend(s_refs[t % PIPE_TILES], m_refs[t % PIPE_TILES], tile_rows(t))
        ahead = t + 2
        slot = ahead % PIPE_TILES
        if ahead < n_tiles:
            scores(q_ref[0, tile_rows(ahead), :], kc_ref, kx_ref, s_refs[slot], m_refs[slot])
        else:
            scores(qn_ref[0, tile_rows(ahead - n_tiles), :], kcn_ref, kxn_ref, s_refs[slot], m_refs[slot])


def _attn_pipe(q, kc, kx, vc, vx, tq, diff_params=None):
    diff = diff_params is not None
    bn, t, _ = q.shape
    nh = DA_HEADS if diff else MLA_HEADS
    hw = LANES if diff else 2 * LANES
    step_rows = ATTN_STEP_TILES * tq
    ns = t // step_rows
    assert t % step_rows == 0 and ATTN_STEP_TILES % PIPE_TILES == 0
    rows = 2 * tq if diff else tq
    nkeys = kc.shape[1] + kx.shape[1]
    nxt = _next_step_maps(bn, nh, ns)

    def q_next(b, h, i):
        b2, h2, i2 = nxt(b, h, i)
        return b2, i2, h2

    def k_next(b, h, i):
        b2, h2, _ = nxt(b, h, i)
        return b2, 0, h2

    cur = lambda b, h, i: (b, 0, h)
    cspec = lambda a: pl.BlockSpec(a.shape, lambda b, h, i: (0, 0))
    extra = list(diff_params[:2]) if diff else []
    lam_init = diff_params[2] if diff else 0.0
    return pl.pallas_call(
        functools.partial(_attn_pipe_kernel, diff, lam_init),
        grid=(bn, nh, ns),
        in_specs=[pl.BlockSpec((1, step_rows, hw), lambda b, h, i: (b, i, h)),
                  pl.BlockSpec((1, step_rows, hw), q_next),
                  pl.BlockSpec((1, kc.shape[1], hw), cur),
                  pl.BlockSpec((1, kx.shape[1], hw), cur),
                  pl.BlockSpec((1, kc.shape[1], hw), k_next),
                  pl.BlockSpec((1, kx.shape[1], hw), k_next),
                  pl.BlockSpec((1, vc.shape[1], 2 * LANES), cur),
                  pl.BlockSpec((1, vx.shape[1], 2 * LANES), cur)]
                 + [cspec(a) for a in extra],
        out_specs=pl.BlockSpec((1, step_rows, LANES), lambda b, h, i: (b, i, h)),
        out_shape=jax.ShapeDtypeStruct((bn, t, nh * LANES), BF16),
        scratch_shapes=[pltpu.VMEM((rows, nkeys), F32)] * PIPE_TILES + [pltpu.VMEM((rows, LANES), F32)] * PIPE_TILES,
        compiler_params=pltpu.CompilerParams(
            dimension_semantics=("arbitrary",) * 3, vmem_limit_bytes=VMEM_LIMIT),
        name="diff_pipe" if diff else "mla_pipe",
    )(q, q, kc, kx, kc, kx, vc, vx, *extra)


def _post_kernel(ny, *refs):
    x_ref = refs[0]
    y_refs = refs[1:1 + ny]
    ga_ref, shf_ref, scf_ref, gf_ref, g_ref = refs[1 + ny:6 + ny]
    wo_refs = refs[6 + ny:6 + 2 * ny]
    win_ref, wout_ref, o_ref, x1_a, x1_b, h_a, h_b = refs[6 + 2 * ny:]
    x1_bufs, h_bufs = (x1_a, x1_b), (h_a, h_b)
    nsub = x_ref.shape[1] // POST_ROWS

    def mix(t):
        rs = slice(t * POST_ROWS, (t + 1) * POST_ROWS)
        attn = functools.reduce(jnp.add, [_dot(y[0, rs, :], w[...]) for y, w in zip(y_refs, wo_refs)])
        x1 = x_ref[0, rs, :] + ga_ref[0, 0] * attn
        x1_bufs[t % 2][...] = x1
        h_bufs[t % 2][...] = _modulated_norm(x1, g_ref[...], shf_ref[0, 0], scf_ref[0, 0]).astype(BF16)

    def ffn(t):
        rs = slice(t * POST_ROWS, (t + 1) * POST_ROWS)
        u = _dot(h_bufs[t % 2][...], win_ref[...])
        gate = u[:, :D_FF]
        act = (gate * jax.nn.sigmoid(gate) * u[:, D_FF:]).astype(BF16)
        o_ref[0, rs, :] = x1_bufs[t % 2][...] + gf_ref[0, 0] * _dot(act, wout_ref[...])

    mix(0)
    for t in range(nsub):
        if t + 1 < nsub:
            mix(t + 1)
        ffn(t)


def _post(x, ys, mods, gate_a, shift_f, scale_f, gate_f, gain, w_os, w_in, w_out, tm):
    bn, t, d = x.shape
    tok = lambda w: pl.BlockSpec((1, tm, w), lambda b, i: (b, i, 0))
    wspec = lambda a: pl.BlockSpec(a.shape, lambda b, i: (0, 0), pipeline_mode=pl.Buffered(1))
    return pl.pallas_call(
        functools.partial(_post_kernel, len(ys)),
        grid=(bn, t // tm),
        in_specs=[tok(d)] + [tok(y.shape[2]) for y in ys]
                 + [_mod_spec(gate_a), _mod_spec(shift_f), _mod_spec(scale_f), _mod_spec(gate_f), _const_spec(gain)]
                 + [wspec(w) for w in w_os] + [wspec(w_in), wspec(w_out)],
        out_specs=tok(d),
        out_shape=jax.ShapeDtypeStruct((bn, t, d), F32),
        scratch_shapes=[pltpu.VMEM((POST_ROWS, d), F32)] * 2 + [pltpu.VMEM((POST_ROWS, d), BF16)] * 2,
        compiler_params=pltpu.CompilerParams(
            dimension_semantics=("arbitrary", "arbitrary"), vmem_limit_bytes=VMEM_LIMIT),
        name="post",
    )(x, *ys, mods, mods, mods, mods, gain, *w_os, w_in, w_out)


def _proj_odd_kernel(*refs):
    for t in range(refs[0].shape[1] // PROJ_ROWS):
        _proj_odd_tile(slice(t * PROJ_ROWS, (t + 1) * PROJ_ROWS), *refs)


def _proj_odd_tile(rs, x_ref, sh_ref, sc_ref, g_ref, w_ref, g64_ref, qg_ref, kg_ref, q_ref, k_ref, v_ref):
    h = _modulated_norm(x_ref[0, rs, :], g_ref[...], sh_ref[0, 0], sc_ref[0, 0]).astype(BF16)
    p = _dot(h, w_ref[...])
    w = NA_HEADS * NA_DH

    def qk(v, gain, scale):
        return (v * _group_rsqrt(v, g64_ref[...], NA_DH) * (gain * scale)).astype(BF16)

    q_ref[0, rs, :] = qk(p[:, :w], qg_ref[...], NA_DH ** -0.5 * LOG2E)
    k_ref[0, rs, :] = qk(p[:, w:2 * w], kg_ref[...], 1.0)
    v = p[:, 2 * w:].astype(BF16)
    ones = jnp.ones((p.shape[0], LANES), BF16)
    v_ref[0, rs, :] = jnp.concatenate(
        [blk for j in range(w // LANES) for blk in (v[:, j * LANES:(j + 1) * LANES], ones)], axis=1)


def _proj_odd(x, mods, shift, scale, gain, consts, tm):
    bn, t, d = x.shape
    tok = lambda w: pl.BlockSpec((1, tm, w), lambda b, i: (b, i, 0))
    w = NA_HEADS * NA_DH
    widths = (w, w, 2 * w)
    return pl.pallas_call(
        _proj_odd_kernel,
        grid=(bn, t // tm),
        in_specs=[tok(d), _mod_spec(shift), _mod_spec(scale), _const_spec(gain)] + [_const_spec(c) for c in consts],
        out_specs=[tok(n) for n in widths],
        out_shape=[jax.ShapeDtypeStruct((bn, t, n), BF16) for n in widths],
        compiler_params=pltpu.CompilerParams(
            dimension_semantics=("arbitrary", "arbitrary"), vmem_limit_bytes=VMEM_LIMIT),
        name="proj_odd",
    )(x, mods, mods, gain, *consts)


def _na_window_start(g):
    return min(max(NA_GROUP * g - NA_KH // 2, 0), NA_ROWS - NA_WIN_ROWS)


def _na_bias_tile(g, bias_r, lo):
    ws = _na_window_start(g)
    neg = jnp.full((GRID_W, LANES), NEG_INF, F32)
    row_tiles = []
    for head in range(2):
        for a in range(NA_GROUP):
            r = NA_GROUP * g + a
            rs = min(max(r - NA_KH // 2, 0), NA_ROWS - NA_KH)
            blocks = []
            for jp in range(NA_WIN_ROWS // 2):
                kr0 = ws + 2 * jp
                ok0 = rs <= kr0 < rs + NA_KH
                ok1 = rs <= kr0 + 1 < rs + NA_KH
                if not (ok0 or ok1):
                    blocks.append(neg)
                    continue
                blk = bias_r[0, head, kr0 - r + NA_KH - 1]
                if not ok0:
                    blk = jnp.where(lo, neg, blk)
                if not ok1:
                    blk = jnp.where(lo, blk, neg)
                blocks.append(blk)
            row_tiles.append(jnp.concatenate(blocks, axis=1))
    return jnp.concatenate(row_tiles, axis=0)


def _na_kernel(q_ref, k_ref, v_ref, kc_ref, vc_ref, bias_ref, qn_ref, kn_ref, kcn_ref, biasn_ref, o_ref,
               *scratch):
    s_refs = scratch[:PIPE_TILES]
    m_refs = scratch[PIPE_TILES:]
    lane = lax.broadcasted_iota(jnp.int32, (1, LANES), 1)
    lo = lane < NA_DH
    gq = NA_GROUP * GRID_W
    nloc = NA_WIN_ROWS * GRID_W
    nk = s_refs[0].shape[1]

    def win(g):
        k0 = _na_window_start(g) * GRID_W
        return slice(k0, k0 + nloc)

    def scores(g, q_r, k_r, kc_r, bias_r, s_ref, m_ref):
        q = q_r[0, g * gq:(g + 1) * gq, :]
        zero = jnp.zeros_like(q)
        qq = jnp.concatenate([jnp.where(lo, q, zero), jnp.where(lo, zero, q)], axis=0)
        s_loc = _dot_nt(qq, k_r[0, win(g), :]) + _na_bias_tile(g, bias_r, lo)
        s_ctx = _dot_nt(qq, kc_r[0])
        s_ref[:, :nloc] = s_loc
        s_ref[:, nloc:] = s_ctx
        m = jnp.maximum(jnp.max(s_loc, axis=-1, keepdims=True), jnp.max(s_ctx, axis=-1, keepdims=True))
        m_ref[...] = _lane_bcast(m, 2 * gq)

    def attend(g, s_ref, m_ref):
        m = jnp.concatenate([m_ref[...]] * 2, axis=1)
        k0 = _na_window_start(g) * GRID_W
        acc = None
        for j in range(nk // MXU_DIM):
            c0 = j * MXU_DIM
            p = jnp.exp2(s_ref[:, c0:c0 + MXU_DIM] - m).astype(BF16)
            v = v_ref[0, k0 + c0:k0 + c0 + MXU_DIM, :] if c0 < nloc else vc_ref[0, c0 - nloc:c0 - nloc + MXU_DIM, :]
            part = _dot(p, v)
            acc = part if acc is None else acc + part
        o = acc[:, :LANES] * (1.0 / acc[:, LANES:])
        o_ref[0, g * gq:(g + 1) * gq, :] = jnp.where(lo, o[:gq], o[gq:]).astype(BF16)

    @pl.when((pl.program_id(0) == 0) & (pl.program_id(1) == 0))
    def _():
        for g in range(2):
            scores(g, q_ref, k_ref, kc_ref, bias_ref, s_refs[g], m_refs[g])

    for g in range(NA_GROUPS):
        attend(g, s_refs[g % PIPE_TILES], m_refs[g % PIPE_TILES])
        ahead = g + 2
        slot = ahead % PIPE_TILES
        if ahead < NA_GROUPS:
            scores(ahead, q_ref, k_ref, kc_ref, bias_ref, s_refs[slot], m_refs[slot])
        else:
            scores(ahead - NA_GROUPS, qn_ref, kn_ref, kcn_ref, biasn_ref, s_refs[slot], m_refs[slot])


def _na_attn(q, k, v, kc, vc, bias):
    bn, s, w = q.shape
    assert s == NA_ROWS * GRID_W and NA_GROUPS % PIPE_TILES == 0
    pairs = w // LANES
    last = pairs * bn - 1

    def nxt(p, b):
        u = jnp.minimum(p * bn + b + 1, last)
        return u % bn, 0, u // bn

    cur = lambda p, b: (b, 0, p)
    spec = lambda a, width, imap: pl.BlockSpec((1, a.shape[1], width), imap)
    rows = 2 * NA_GROUP * GRID_W
    nk = NA_WIN_ROWS * GRID_W + kc.shape[1]
    return pl.pallas_call(
        _na_kernel,
        grid=(pairs, bn),
        in_specs=[spec(q, LANES, cur), spec(k, LANES, cur), spec(v, 2 * LANES, cur),
                  spec(kc, LANES, cur), spec(vc, 2 * LANES, cur),
                  pl.BlockSpec((1,) + bias.shape[1:], lambda p, b: (p, 0, 0, 0, 0)),
                  spec(q, LANES, nxt), spec(k, LANES, nxt), spec(kc, LANES, nxt),
                  pl.BlockSpec((1,) + bias.shape[1:], lambda p, b: (nxt(p, b)[2], 0, 0, 0, 0))],
        out_specs=spec(q, LANES, cur),
        out_shape=jax.ShapeDtypeStruct((bn, s, w), BF16),
        scratch_shapes=[pltpu.VMEM((rows, nk), F32)] * PIPE_TILES + [pltpu.VMEM((rows, LANES), F32)] * PIPE_TILES,
        compiler_params=pltpu.CompilerParams(
            dimension_semantics=("arbitrary", "arbitrary"), vmem_limit_bytes=VMEM_LIMIT),
        name="na_attn",
    )(q, k, v, kc, vc, bias, q, k, kc, bias)


def _group_matrix(width, group):
    idx = np.arange(width) // group
    return jnp.asarray((idx[:, None] == idx[None, :]).astype(np.float32), dtype=BF16)


def _rope_tables(s):
    t = np.arange(s)
    row = (t // GRID_W).astype(np.float32)
    col = (t % GRID_W).astype(np.float32)
    half = DA_DK // 2
    inv = (np.float32(ROPE_THETA) ** (-np.arange(0, half, 2, dtype=np.float32) / np.float32(half))).astype(np.float32)
    ar = row[:, None] * inv
    ac = col[:, None] * inv
    cr, sr, cc, sc = np.cos(ar), np.sin(ar), np.cos(ac), np.sin(ac)
    cos64 = np.concatenate([cr, cr, cc, cc], axis=1)
    sin64 = np.concatenate([-sr, sr, -sc, sc], axis=1)
    return (jnp.asarray(np.concatenate([cos64, cos64], axis=1), F32),
            jnp.asarray(np.concatenate([sin64, sin64], axis=1), F32))


def _na_bias_table(rpb):
    cols = np.arange(GRID_W)
    col_start = np.clip(cols - NA_KW // 2, 0, GRID_W - NA_KW)
    col_ok = (cols[None, :] >= col_start[:, None]) & (cols[None, :] < col_start[:, None] + NA_KW)
    dc_idx = np.clip(cols[None, :] - cols[:, None], -(NA_KW - 1), NA_KW - 1) + NA_KW - 1
    by_col = jnp.where(jnp.asarray(col_ok)[None, None], rpb[:, :, dc_idx] * LOG2E, NEG_INF)
    pairs = jnp.concatenate([by_col[:, :-1], by_col[:, 1:]], axis=-1)
    return pairs.reshape(NA_HEADS // 2, 2, 2 * NA_KH - 2, GRID_W, 2 * GRID_W)


def _tile_lanes(g, n):
    return jnp.tile(g, n).reshape(1, -1)


def kernel(x, c, ctx, c_ctx, mod_w, mod_b, norm_mix_g, norm_ffn_g, w_out, ffn_w_in, ffn_w_out, ev_w_in, da_q_g, da_k_g, da_lq1, da_lk1, da_lq2, da_lk2, da_out_g, mla_q_a_g, mla_w_uq, mla_kv_a_g, mla_w_ukv, mla_q_g, mla_k_g, mla_kr_g, od_w_in, na_q_g, na_k_g, na_rpb):
    bsz, seq, d = x.shape
    tm = 256
    tm_x = 2 * PROJ_ROWS
    tq = 256

    cond = jnp.concatenate([c, c_ctx[None], jnp.zeros((7, d), F32)], axis=0)
    mods = _adaln(cond, mod_w, mod_b)

    def mod_vectors(l, row):
        return [(l, row, j) for j in range(6)]

    g64 = _group_matrix(MXU_DIM, 64)
    cos, sin = _rope_tables(seq)

    l = 0
    w_in = jnp.concatenate([ev_w_in[0], jnp.zeros((d, 64), F32)], axis=1).astype(BF16)
    wuq = mla_w_uq[0].reshape(MLA_Q_LORA, MLA_HEADS, MLA_NOPE + MLA_ROPE)
    wuq = jnp.pad(wuq, ((0, 0), (0, 0), (0, 64))).reshape(MLA_Q_LORA, MLA_HEADS * 256).astype(BF16)
    mqg = jnp.tile(jnp.pad(mla_q_g[0], (0, 64)), MLA_HEADS).reshape(1, -1)
    consts = [
        w_in, g64, _group_matrix(LANES, LANES), _group_matrix(MXU_DIM, MXU_DIM),
        _tile_lanes(da_q_g[0], 8), _tile_lanes(da_k_g[0], 8),
        mla_q_a_g[0].reshape(1, -1), wuq, mla_kv_a_g[0].reshape(1, -1), mla_w_ukv[0].astype(BF16),
        mqg, mla_k_g[0].reshape(1, -1), jnp.pad(mla_kr_g[0], (0, 64)).reshape(1, -1),
    ]
    lqk = jnp.stack([da_lq1[0], da_lk1[0], da_lq2[0], da_lk2[0]])
    lam_init = 0.8 - 0.6 * math.exp(-0.3 * l)
    out_g = da_out_g[0].reshape(1, -1)
    w_o = w_out[l].astype(BF16)
    w_os = [w_o[:DA_HEADS * DA_DV], w_o[DA_HEADS * DA_DV:]]
    f_in = ffn_w_in[l].astype(BF16)
    f_out = ffn_w_out[l].astype(BF16)
    gm = norm_mix_g[l].reshape(1, d)
    gf = norm_ffn_g[l].reshape(1, d)

    sh_a, sc_a, g_a, sh_f, sc_f, g_f = mod_vectors(l, None)
    csh_a, csc_a, cg_a, csh_f, csc_f, cg_f = mod_vectors(l, bsz)

    px = _proj_even(x, mods, sh_a, sc_a, gm, consts, cos, sin, True, tm_x)
    pc = _proj_even(ctx, mods, csh_a, csc_a, gm, consts, cos[:ctx.shape[1]], sin[:ctx.shape[1]], False, tm)
    dq_x, dk_x, dv_x, mq_x, mk_x, mv_x = px
    dq_c, dk_c, dv_c, mq_c, mk_c, mv_c = pc

    da_x = _attn_pipe(dq_x, dk_c, dk_x, dv_c, dv_x, tq, (lqk, out_g, lam_init))
    mla_x = _attn_pipe(mq_x, mk_c, mk_x, mv_c, mv_x, tq)
    da_c = _ctx_diff_attn(dq_c, dk_c, dv_c, lqk, out_g, lam_init)
    mla_c = _ctx_mla_attn(mq_c, mk_c, mv_c)

    x = _post(x, [da_x, mla_x], mods, g_a, sh_f, sc_f, g_f, gf, w_os, f_in, f_out, POST_STEP_TILES * POST_ROWS)
    ctx = _post(ctx, [da_c, mla_c], mods, cg_a, csh_f, csc_f, cg_f, gf, w_os, f_in, f_out, tm)

    l = 1
    sh_a, sc_a, g_a, sh_f, sc_f, g_f = mod_vectors(l, None)
    csh_a, csc_a = mod_vectors(l, bsz)[:2]
    gm = norm_mix_g[l].reshape(1, d)
    gf = norm_ffn_g[l].reshape(1, d)
    consts = [od_w_in[0].astype(BF16), g64, _tile_lanes(na_q_g[0], NA_HEADS), _tile_lanes(na_k_g[0], NA_HEADS)]
    q_x, k_x, v_x = _proj_odd(x, mods, sh_a, sc_a, gm, consts, tm_x)
    _, k_c, v_c = _proj_odd(ctx, mods, csh_a, csc_a, gm, consts, tm)
    y = _na_attn(q_x, k_x, v_x, k_c, v_c, _na_bias_table(na_rpb[0]))
    x = _post(x, [y], mods, g_a, sh_f, sc_f, g_f, gf, [w_out[l].astype(BF16)],
              ffn_w_in[l].astype(BF16), ffn_w_out[l].astype(BF16), POST_STEP_TILES * POST_ROWS)
    return x
```

```python
import functools
import math

import jax
import jax.numpy as jnp
import numpy as np
from jax import lax
from jax.experimental import pallas as pl
from jax.experimental.pallas import tpu as pltpu

D_MODEL = 1024
DEPTH = 2
GRID_W = 64
DA_HEADS = 4
DA_DK = 64
DA_DV = 2 * DA_DK
MLA_HEADS = 4
MLA_NOPE = 128
MLA_ROPE = 64
MLA_V = 128
MLA_Q_LORA = 256
MLA_KV_LORA = 128
NA_HEADS = 16
NA_DH = 64
NA_KH = 8
NA_KW = 16
D_FF = -(-8 * D_MODEL // (3 * 256)) * 256
ROPE_THETA = 10000.0
EPS = 1e-6
NEG_INF = -1e30
LOG2E = math.log2(math.e)
SEQ = 2048
NA_ROWS = SEQ // GRID_W
NA_GROUP = 4
NA_GROUPS = NA_ROWS // NA_GROUP
NA_STEP_BATCH = 2
NA_WIN_ROWS = NA_GROUP + NA_KH

LANES = 128
MXU_DIM = 256
VMEM_LIMIT = 56 * 1024 * 1024

PROJ_ROWS = 256
POST_STEP_TILES = 4
POST_ROWS = 256
PIPE_TILES = 4
ATTN_STEP_TILES = 8

BF16 = jnp.bfloat16
F32 = jnp.float32


def _dot(a, b):
    return jnp.dot(a, b, preferred_element_type=F32)


def _dot_nt(a, b):
    return lax.dot_general(a, b, (((1,), (1,)), ((), ())), preferred_element_type=F32)


def _group_rsqrt(x, gmat, size):
    x2 = (x * x).astype(BF16)
    w = gmat.shape[0]
    n = x2.shape[1] // w
    sums = [_dot(x2[:, i * w:(i + 1) * w], gmat) for i in range(n)]
    ss = sums[0] if n == 1 else jnp.concatenate(sums, axis=1)
    return lax.rsqrt(ss * (1.0 / size) + EPS)


def _swap16(x):
    n = x.shape[1]
    lane = lax.broadcasted_iota(jnp.int32, (1, n), 1)
    up = pltpu.roll(x, n - 16, 1)
    down = pltpu.roll(x, 16, 1)
    return jnp.where((lane % 32) < 16, up, down)


def _rope(x, cos, sin_signed):
    return x * cos + _swap16(x) * sin_signed


def _modulated_norm(x, g, shift, scale):
    y = x * lax.rsqrt(jnp.mean(x * x, axis=-1, keepdims=True) + EPS)
    return (y * g) * (1.0 + scale) + shift


def _adaln_kernel(cond_ref, w_ref, b_ref, o_ref):
    c = cond_ref[...]
    a = (c * jax.nn.sigmoid(c)).astype(BF16)
    res = _dot(a, w_ref[0].astype(BF16)) + b_ref[0]
    for r in range(res.shape[0]):
        o_ref[0, r] = res[r:r + 1, :]


def _adaln(cond, mod_w, mod_b):
    rows = cond.shape[0]
    n = mod_w.shape[2]
    tn = 1536
    return pl.pallas_call(
        _adaln_kernel,
        grid=(DEPTH, n // tn),
        in_specs=[
            pl.BlockSpec((rows, D_MODEL), lambda l, j: (0, 0)),
            pl.BlockSpec((1, D_MODEL, tn), lambda l, j: (l, 0, j)),
            pl.BlockSpec((1, 1, tn), lambda l, j: (l, 0, j)),
        ],
        out_specs=pl.BlockSpec((1, rows, 1, tn), lambda l, j: (l, 0, 0, j)),
        out_shape=jax.ShapeDtypeStruct((DEPTH, rows, 1, n), F32),
        compiler_params=pltpu.CompilerParams(
            dimension_semantics=("arbitrary", "arbitrary"), vmem_limit_bytes=VMEM_LIMIT),
        name="adaln",
    )(cond, mod_w, mod_b.reshape(DEPTH, 1, n))


def _proj_even_kernel(rope, *refs):
    for t in range(refs[0].shape[1] // PROJ_ROWS):
        _proj_even_tile(rope, slice(t * PROJ_ROWS, (t + 1) * PROJ_ROWS), *refs)


def _proj_even_tile(rope, rs, x_ref, sh_ref, sc_ref, g_ref, w_ref, g64_ref, g128_ref, g256_ref,
                    dqg_ref, dkg_ref, qag_ref, wuq_ref, kvag_ref, wukv_ref, mqg_ref, mkg_ref,
                    krg_ref, cos_ref, sin_ref,
                    dq_ref, dk_ref, dv_ref, mq_ref, mk_ref, mv_ref):
    h = _modulated_norm(x_ref[0, rs, :], g_ref[...], sh_ref[0, 0], sc_ref[0, 0]).astype(BF16)
    p = _dot(h, w_ref[...])
    n_dq = DA_HEADS * 2 * DA_DK
    if rope:
        cos128 = cos_ref[rs, :]
        sin128 = sin_ref[rs, :]
        cos512 = jnp.concatenate([cos128] * 4, axis=1)
        sin512 = jnp.concatenate([sin128] * 4, axis=1)
        lane = lax.broadcasted_iota(jnp.int32, (1, LANES), 1)
        cos_half = jnp.where(lane < MLA_ROPE, cos128, 1.0)
        sin_half = jnp.where(lane < MLA_ROPE, sin128, 0.0)

    def da_qk(v, gain, scale):
        y = v * _group_rsqrt(v, g64_ref[...], DA_DK) * (gain * scale)
        if rope:
            y = _rope(y, cos512, sin512)
        return y.astype(BF16)

    dq_ref[0, rs, :] = da_qk(p[:, 0:n_dq], dqg_ref[...], DA_DK ** -0.5 * LOG2E)
    dk_ref[0, rs, :] = da_qk(p[:, n_dq:2 * n_dq], dkg_ref[...], 1.0)
    ones = jnp.ones((p.shape[0], LANES), BF16)
    dv = p[:, 2 * n_dq:3 * n_dq].astype(BF16)
    dv_ref[0, rs, :] = jnp.concatenate(
        [blk for hh in range(DA_HEADS) for blk in (dv[:, hh * DA_DV:(hh + 1) * DA_DV], ones)], axis=1)

    c0 = 3 * n_dq
    cq = p[:, c0:c0 + MLA_Q_LORA]
    cq = cq * _group_rsqrt(cq, g256_ref[...], MLA_Q_LORA) * qag_ref[...]
    q = _dot(cq.astype(BF16), wuq_ref[...])
    q = q * _group_rsqrt(q, g256_ref[...], MLA_NOPE + MLA_ROPE) * (
        mqg_ref[...] * ((MLA_NOPE + MLA_ROPE) ** -0.5 * LOG2E))

    c1 = c0 + MLA_Q_LORA
    ckv = p[:, c1:c1 + MLA_KV_LORA]
    ckv = ckv * _group_rsqrt(ckv, g128_ref[...], MLA_KV_LORA) * kvag_ref[...]
    kv = _dot(ckv.astype(BF16), wukv_ref[...])

    c2 = c1 + MLA_KV_LORA
    kr = p[:, c2:c2 + LANES]
    kr = kr * _group_rsqrt(kr, g128_ref[...], MLA_ROPE) * krg_ref[...]
    if rope:
        kr = _rope(kr, cos_half, sin_half)
    kr = kr.astype(BF16)

    mq, mk, mv = [], [], []
    for hh in range(MLA_HEADS):
        b0 = hh * 2 * LANES
        qn = q[:, b0:b0 + LANES]
        qr = q[:, b0 + LANES:b0 + 2 * LANES]
        if rope:
            qr = _rope(qr, cos_half, sin_half)
        mq += [qn.astype(BF16), qr.astype(BF16)]
        kn = kv[:, b0:b0 + LANES]
        kn = kn * _group_rsqrt(kn, g128_ref[...], MLA_NOPE) * mkg_ref[...]
        mk += [kn.astype(BF16), kr]
        mv += [kv[:, b0 + LANES:b0 + 2 * LANES].astype(BF16), ones]
    mq_ref[0, rs, :] = jnp.concatenate(mq, axis=1)
    mk_ref[0, rs, :] = jnp.concatenate(mk, axis=1)
    mv_ref[0, rs, :] = jnp.concatenate(mv, axis=1)


def _const_spec(a):
    nd = a.ndim
    return pl.BlockSpec(a.shape, lambda b, i: (0,) * nd)


def _mod_spec(vec):
    l, row, j = vec
    return pl.BlockSpec((1, 1, 1, D_MODEL), lambda b, i: (l, b if row is None else row, 0, j))


def _proj_even(x, mods, shift, scale, gain, consts, cos, sin, rope, tm):
    bn, t, d = x.shape
    tok = lambda w: pl.BlockSpec((1, tm, w), lambda b, i: (b, i, 0))
    rope_spec = pl.BlockSpec((tm, LANES), lambda b, i: (i, 0))
    widths = (512, 512, 1024, 1024, 1024, 1024)
    return pl.pallas_call(
        functools.partial(_proj_even_kernel, rope),
        grid=(bn, t // tm),
        in_specs=[tok(d), _mod_spec(shift), _mod_spec(scale), _const_spec(gain)]
                 + [_const_spec(c) for c in consts] + [rope_spec, rope_spec],
        out_specs=[tok(w) for w in widths],
        out_shape=[jax.ShapeDtypeStruct((bn, t, w), BF16) for w in widths],
        compiler_params=pltpu.CompilerParams(
            dimension_semantics=("arbitrary", "arbitrary"), vmem_limit_bytes=VMEM_LIMIT),
        name="proj_even",
    )(x, mods, mods, gain, *consts, cos, sin)


def _softmax_parts(s_list):
    m = functools.reduce(jnp.maximum, [jnp.max(s, axis=-1, keepdims=True) for s in s_list])
    e_list = [jnp.exp2(s - m) for s in s_list]
    l = functools.reduce(jnp.add, [jnp.sum(e, axis=-1, keepdims=True) for e in e_list])
    return e_list, l


def _ctx_diff_attn_kernel(lam_init, q_ref, k_ref, v_ref, lqk_ref, og_ref, o_ref):
    lqk = lqk_ref[...]
    lam = (jnp.exp(jnp.sum(lqk[0:1] * lqk[1:2], axis=-1, keepdims=True))
           - jnp.exp(jnp.sum(lqk[2:3] * lqk[3:4], axis=-1, keepdims=True)) + lam_init)
    lane = lax.broadcasted_iota(jnp.int32, (1, LANES), 1)
    for h in range(DA_HEADS):
        hs = slice(h * LANES, (h + 1) * LANES)
        q = q_ref[0, :, hs]
        k = k_ref[0, :, hs]
        v = v_ref[0, :, 2 * h * LANES:(2 * h + 1) * LANES]
        zero = jnp.zeros_like(q)
        (e1,), l1 = _softmax_parts([_dot_nt(jnp.where(lane < DA_DK, q, zero), k)])
        (e2,), l2 = _softmax_parts([_dot_nt(jnp.where(lane < DA_DK, zero, q), k)])
        o = _dot((e1 * (1.0 / l1) - e2 * (lam / l2)).astype(BF16), v)
        o = o * lax.rsqrt(jnp.mean(o * o, axis=-1, keepdims=True) + EPS) * (og_ref[...] * (1.0 - lam_init))
        o_ref[0, :, hs] = o.astype(BF16)


def _ctx_diff_attn(q, k, v, lqk, out_g, lam_init):
    bn, n, _ = q.shape
    spec = lambda a: pl.BlockSpec((1, n, a.shape[2]), lambda b: (b, 0, 0))
    cspec = lambda a: pl.BlockSpec(a.shape, lambda b: (0, 0))
    return pl.pallas_call(
        functools.partial(_ctx_diff_attn_kernel, lam_init),
        grid=(bn,),
        in_specs=[spec(q), spec(k), spec(v), cspec(lqk), cspec(out_g)],
        out_specs=spec(q),
        out_shape=jax.ShapeDtypeStruct((bn, n, DA_HEADS * DA_DV), BF16),
        compiler_params=pltpu.CompilerParams(dimension_semantics=("arbitrary",), vmem_limit_bytes=VMEM_LIMIT),
        name="ctx_diff_attn",
    )(q, k, v, lqk, out_g)


def _ctx_mla_attn_kernel(q_ref, k_ref, v_ref, o_ref):
    hw = 2 * LANES
    for h in range(MLA_HEADS):
        q = q_ref[0, :, h * hw:(h + 1) * hw]
        k = k_ref[0, :, h * hw:(h + 1) * hw]
        v = v_ref[0, :, h * hw:h * hw + LANES]
        (e,), l = _softmax_parts([_dot_nt(q, k)])
        o_ref[0, :, h * LANES:(h + 1) * LANES] = (_dot(e.astype(BF16), v) * (1.0 / l)).astype(BF16)


def _ctx_mla_attn(q, k, v):
    bn, n, _ = q.shape
    spec = lambda a: pl.BlockSpec((1, n, a.shape[2]), lambda b: (b, 0, 0))
    return pl.pallas_call(
        _ctx_mla_attn_kernel,
        grid=(bn,),
        in_specs=[spec(q), spec(k), spec(v)],
        out_specs=pl.BlockSpec((1, n, MLA_HEADS * MLA_V), lambda b: (b, 0, 0)),
        out_shape=jax.ShapeDtypeStruct((bn, n, MLA_HEADS * MLA_V), BF16),
        compiler_params=pltpu.CompilerParams(dimension_semantics=("arbitrary",), vmem_limit_bytes=VMEM_LIMIT),
        name="ctx_mla_attn",
    )(q, k, v)


def _next_step_maps(nb, nh, ns):
    last = nb * nh * ns - 1

    def nxt(b, h, i):
        u = jnp.minimum((b * nh + h) * ns + i + 1, last)
        return u // (nh * ns), (u // ns) % nh, u % ns

    return nxt


def _is_first_step():
    return (pl.program_id(0) == 0) & (pl.program_id(1) == 0) & (pl.program_id(2) == 0)


def _lane_bcast(col, rows):
    return jnp.broadcast_to(col, (rows, LANES))


def _attn_pipe_kernel(diff, lam_init, *refs):
    n_in = 10 if diff else 8
    q_ref, qn_ref, kc_ref, kx_ref, kcn_ref, kxn_ref, vc_ref, vx_ref = refs[:8]
    o_ref = refs[n_in]
    s_refs = refs[n_in + 1:n_in + 1 + PIPE_TILES]
    m_refs = refs[n_in + 1 + PIPE_TILES:]
    rows = s_refs[0].shape[0]
    tq = rows // 2 if diff else rows
    nc = kc_ref.shape[1]
    nk = s_refs[0].shape[1]
    lane = lax.broadcasted_iota(jnp.int32, (1, LANES), 1)

    def scores(q, kc, kx, s_ref, m_ref):
        if diff:
            zero = jnp.zeros_like(q)
            q = jnp.concatenate([jnp.where(lane < DA_DK, q, zero), jnp.where(lane < DA_DK, zero, q)], axis=0)
        sc = _dot_nt(q, kc[0])
        sx = _dot_nt(q, kx[0])
        s_ref[:, :nc] = sc
        s_ref[:, nc:] = sx
        m = jnp.maximum(jnp.max(sc, axis=-1, keepdims=True), jnp.max(sx, axis=-1, keepdims=True))
        m_ref[...] = _lane_bcast(m, rows)

    def attend(s_ref, m_ref, out_rows):
        m = jnp.concatenate([m_ref[...]] * 2, axis=1)
        acc = None
        for j in range(nk // MXU_DIM):
            k0 = j * MXU_DIM
            p = jnp.exp2(s_ref[:, k0:k0 + MXU_DIM] - m).astype(BF16)
            v = vc_ref[0, k0:k0 + MXU_DIM, :] if k0 < nc else vx_ref[0, k0 - nc:k0 - nc + MXU_DIM, :]
            part = _dot(p, v)
            acc = part if acc is None else acc + part
        o = acc[:, :LANES] * (1.0 / acc[:, LANES:])
        if diff:
            lqk_ref, og_ref = refs[8:10]
            lqk = lqk_ref[...]
            lam = (jnp.exp(jnp.sum(lqk[0:1] * lqk[1:2], axis=-1, keepdims=True))
                   - jnp.exp(jnp.sum(lqk[2:3] * lqk[3:4], axis=-1, keepdims=True)) + lam_init)
            o = o[:tq] - lam * o[tq:]
            o = o * lax.rsqrt(jnp.mean(o * o, axis=-1, keepdims=True) + EPS) * (og_ref[...] * (1.0 - lam_init))
        o_ref[0, out_rows, :] = o.astype(BF16)

    def tile_rows(t):
        return slice(t * tq, (t + 1) * tq)

    @pl.when(_is_first_step())
    def _():
        for t in range(2):
            scores(q_ref[0, tile_rows(t), :], kc_ref, kx_ref, s_refs[t], m_refs[t])

    n_tiles = q_ref.shape[1] // tq
    for t in range(n_tiles):
        attend(s_refs[t % PIPE_TILES], m_refs[t % PIPE_TILES], tile_rows(t))
        ahead = t + 2
        slot = ahead % PIPE_TILES
        if ahead < n_tiles:
            scores(q_ref[0, tile_rows(ahead), :], kc_ref, kx_ref, s_refs[slot], m_refs[slot])
        else:
            scores(qn_ref[0, tile_rows(ahead - n_tiles), :], kcn_ref, kxn_ref, s_refs[slot], m_refs[slot])


def _attn_pipe(q, kc, kx, vc, vx, tq, diff_params=None):
    diff = diff_params is not None
    bn, t, _ = q.shape
    nh = DA_HEADS if diff else MLA_HEADS
    hw = LANES if diff else 2 * LANES
    step_rows = min(ATTN_STEP_TILES * tq, t)
    ns = t // step_rows
    assert t % step_rows == 0 and (step_rows // tq) % PIPE_TILES == 0
    rows = 2 * tq if diff else tq
    nkeys = kc.shape[1] + kx.shape[1]
    nxt = _next_step_maps(bn, nh, ns)

    def q_next(b, h, i):
        b2, h2, i2 = nxt(b, h, i)
        return b2, i2, h2

    def k_next(b, h, i):
        b2, h2, _ = nxt(b, h, i)
        return b2, 0, h2

    cur = lambda b, h, i: (b, 0, h)
    cspec = lambda a: pl.BlockSpec(a.shape, lambda b, h, i: (0, 0))
    extra = list(diff_params[:2]) if diff else []
    lam_init = diff_params[2] if diff else 0.0
    return pl.pallas_call(
        functools.partial(_attn_pipe_kernel, diff, lam_init),
        grid=(bn, nh, ns),
        in_specs=[pl.BlockSpec((1, step_rows, hw), lambda b, h, i: (b, i, h)),
                  pl.BlockSpec((1, step_rows, hw), q_next),
                  pl.BlockSpec((1, kc.shape[1], hw), cur),
                  pl.BlockSpec((1, kx.shape[1], hw), cur),
                  pl.BlockSpec((1, kc.shape[1], hw), k_next),
                  pl.BlockSpec((1, kx.shape[1], hw), k_next),
                  pl.BlockSpec((1, vc.shape[1], 2 * LANES), cur),
                  pl.BlockSpec((1, vx.shape[1], 2 * LANES), cur)]
                 + [cspec(a) for a in extra],
        out_specs=pl.BlockSpec((1, step_rows, LANES), lambda b, h, i: (b, i, h)),
        out_shape=jax.ShapeDtypeStruct((bn, t, nh * LANES), BF16),
        scratch_shapes=[pltpu.VMEM((rows, nkeys), F32)] * PIPE_TILES + [pltpu.VMEM((rows, LANES), F32)] * PIPE_TILES,
        compiler_params=pltpu.CompilerParams(
            dimension_semantics=("arbitrary",) * 3, vmem_limit_bytes=VMEM_LIMIT),
        name="diff_pipe" if diff else "mla_pipe",
    )(q, q, kc, kx, kc, kx, vc, vx, *extra)


def _post_kernel(ny, *refs):
    x_ref = refs[0]
    y_refs = refs[1:1 + ny]
    ga_ref, shf_ref, scf_ref, gf_ref, g_ref, wo_ref = refs[1 + ny:7 + ny]
    win_ref, wout_ref, o_ref, x1_a, x1_b, h_a, h_b = refs[7 + ny:]
    x1_bufs, h_bufs = (x1_a, x1_b), (h_a, h_b)
    nsub = x_ref.shape[1] // POST_ROWS
    w_rows = np.cumsum([0] + [y.shape[2] for y in y_refs])

    def mix(t):
        rs = slice(t * POST_ROWS, (t + 1) * POST_ROWS)
        attn = functools.reduce(jnp.add, [_dot(y[0, rs, :], wo_ref[int(w_rows[j]):int(w_rows[j + 1]), :])
                                          for j, y in enumerate(y_refs)])
        x1 = x_ref[0, rs, :] + ga_ref[0, 0] * attn
        x1_bufs[t % 2][...] = x1
        h_bufs[t % 2][...] = _modulated_norm(x1, g_ref[...], shf_ref[0, 0], scf_ref[0, 0]).astype(BF16)

    def ffn(t):
        rs = slice(t * POST_ROWS, (t + 1) * POST_ROWS)
        u = _dot(h_bufs[t % 2][...], win_ref[...])
        gate = u[:, :D_FF]
        act = (gate * jax.nn.sigmoid(gate) * u[:, D_FF:]).astype(BF16)
        o_ref[0, rs, :] = x1_bufs[t % 2][...] + gf_ref[0, 0] * _dot(act, wout_ref[...])

    mix(0)
    for t in range(nsub):
        if t + 1 < nsub:
            mix(t + 1)
        ffn(t)


def _post(x, ys, mods, gate_a, shift_f, scale_f, gate_f, gain, w_o, w_in, w_out, tm):
    bn, t, d = x.shape
    tok = lambda w: pl.BlockSpec((1, tm, w), lambda b, i: (b, i, 0))
    wspec = lambda a: pl.BlockSpec(a.shape, lambda b, i: (0, 0), pipeline_mode=pl.Buffered(1))
    return pl.pallas_call(
        functools.partial(_post_kernel, len(ys)),
        grid=(bn, t // tm),
        in_specs=[tok(d)] + [tok(y.shape[2]) for y in ys]
                 + [_mod_spec(gate_a), _mod_spec(shift_f), _mod_spec(scale_f), _mod_spec(gate_f), _const_spec(gain)]
                 + [wspec(w_o), wspec(w_in), wspec(w_out)],
        out_specs=tok(d),
        out_shape=jax.ShapeDtypeStruct((bn, t, d), F32),
        scratch_shapes=[pltpu.VMEM((POST_ROWS, d), F32)] * 2 + [pltpu.VMEM((POST_ROWS, d), BF16)] * 2,
        compiler_params=pltpu.CompilerParams(
            dimension_semantics=("arbitrary", "arbitrary"), vmem_limit_bytes=VMEM_LIMIT),
        name="post",
    )(x, *ys, mods, mods, mods, mods, gain, w_o, w_in, w_out)


def _proj_odd_kernel(*refs):
    for t in range(refs[0].shape[1] // PROJ_ROWS):
        _proj_odd_tile(slice(t * PROJ_ROWS, (t + 1) * PROJ_ROWS), *refs)


def _proj_odd_tile(rs, x_ref, sh_ref, sc_ref, g_ref, w_ref, g64_ref, qg_ref, kg_ref, q_ref, k_ref, v_ref):
    h = _modulated_norm(x_ref[0, rs, :], g_ref[...], sh_ref[0, 0], sc_ref[0, 0]).astype(BF16)
    p = _dot(h, w_ref[...])
    w = NA_HEADS * NA_DH

    def qk(v, gain, scale):
        return (v * _group_rsqrt(v, g64_ref[...], NA_DH) * (gain * scale)).astype(BF16)

    q_ref[0, rs, :] = qk(p[:, :w], qg_ref[...], NA_DH ** -0.5 * LOG2E)
    k_ref[0, rs, :] = qk(p[:, w:2 * w], kg_ref[...], 1.0)
    v = p[:, 2 * w:].astype(BF16)
    ones = jnp.ones((p.shape[0], LANES), BF16)
    v_ref[0, rs, :] = jnp.concatenate(
        [blk for j in range(w // LANES) for blk in (v[:, j * LANES:(j + 1) * LANES], ones)], axis=1)


def _proj_odd(x, mods, shift, scale, gain, consts, tm):
    bn, t, d = x.shape
    tok = lambda w: pl.BlockSpec((1, tm, w), lambda b, i: (b, i, 0))
    w = NA_HEADS * NA_DH
    widths = (w, w, 2 * w)
    return pl.pallas_call(
        _proj_odd_kernel,
        grid=(bn, t // tm),
        in_specs=[tok(d), _mod_spec(shift), _mod_spec(scale), _const_spec(gain)] + [_const_spec(c) for c in consts],
        out_specs=[tok(n) for n in widths],
        out_shape=[jax.ShapeDtypeStruct((bn, t, n), BF16) for n in widths],
        compiler_params=pltpu.CompilerParams(
            dimension_semantics=("arbitrary", "arbitrary"), vmem_limit_bytes=VMEM_LIMIT),
        name="proj_odd",
    )(x, mods, mods, gain, *consts)


def _na_window_start(g):
    return min(max(NA_GROUP * g - NA_KH // 2, 0), NA_ROWS - NA_WIN_ROWS)


def _na_bias_tile(g, bias_r, lo):
    ws = _na_window_start(g)
    neg = jnp.full((GRID_W, LANES), NEG_INF, F32)
    row_tiles = []
    for head in range(2):
        for a in range(NA_GROUP):
            r = NA_GROUP * g + a
            rs = min(max(r - NA_KH // 2, 0), NA_ROWS - NA_KH)
            blocks = []
            for jp in range(NA_WIN_ROWS // 2):
                kr0 = ws + 2 * jp
                ok0 = rs <= kr0 < rs + NA_KH
                ok1 = rs <= kr0 + 1 < rs + NA_KH
                if not (ok0 or ok1):
                    blocks.append(neg)
                    continue
                blk = bias_r[0, head, kr0 - r + NA_KH - 1]
                if not ok0:
                    blk = jnp.where(lo, neg, blk)
                if not ok1:
                    blk = jnp.where(lo, blk, neg)
                blocks.append(blk)
            row_tiles.append(jnp.concatenate(blocks, axis=1))
    return jnp.concatenate(row_tiles, axis=0)


def _na_kernel(q_ref, k_ref, v_ref, kc_ref, vc_ref, bias_ref, qn_ref, kn_ref, kcn_ref, biasn_ref, o_ref,
               *scratch):
    s_refs = scratch[:PIPE_TILES]
    m_refs = scratch[PIPE_TILES:]
    lane = lax.broadcasted_iota(jnp.int32, (1, LANES), 1)
    lo = lane < NA_DH
    gq = NA_GROUP * GRID_W
    nloc = NA_WIN_ROWS * GRID_W
    nk = s_refs[0].shape[1]

    def win(g):
        k0 = _na_window_start(g) * GRID_W
        return slice(k0, k0 + nloc)

    def scores(bi, g, q_r, k_r, kc_r, bias_r, s_ref, m_ref):
        q = q_r[bi, g * gq:(g + 1) * gq, :]
        zero = jnp.zeros_like(q)
        qq = jnp.concatenate([jnp.where(lo, q, zero), jnp.where(lo, zero, q)], axis=0)
        s_loc = _dot_nt(qq, k_r[bi, win(g), :]) + _na_bias_tile(g, bias_r, lo)
        s_ctx = _dot_nt(qq, kc_r[bi])
        s_ref[:, :nloc] = s_loc
        s_ref[:, nloc:] = s_ctx
        m = jnp.maximum(jnp.max(s_loc, axis=-1, keepdims=True), jnp.max(s_ctx, axis=-1, keepdims=True))
        m_ref[...] = _lane_bcast(m, 2 * gq)

    def attend(bi, g, s_ref, m_ref):
        m = jnp.concatenate([m_ref[...]] * 2, axis=1)
        k0 = _na_window_start(g) * GRID_W
        acc = None
        for j in range(nk // MXU_DIM):
            c0 = j * MXU_DIM
            p = jnp.exp2(s_ref[:, c0:c0 + MXU_DIM] - m).astype(BF16)
            v = v_ref[bi, k0 + c0:k0 + c0 + MXU_DIM, :] if c0 < nloc else vc_ref[bi, c0 - nloc:c0 - nloc + MXU_DIM, :]
            part = _dot(p, v)
            acc = part if acc is None else acc + part
        o = acc[:, :LANES] * (1.0 / acc[:, LANES:])
        o_ref[bi, g * gq:(g + 1) * gq, :] = jnp.where(lo, o[:gq], o[gq:]).astype(BF16)

    @pl.when((pl.program_id(0) == 0) & (pl.program_id(1) == 0))
    def _():
        for g in range(2):
            scores(0, g, q_ref, k_ref, kc_ref, bias_ref, s_refs[g], m_refs[g])

    n_groups = q_ref.shape[0] * NA_GROUPS
    for gg in range(n_groups):
        attend(gg // NA_GROUPS, gg % NA_GROUPS, s_refs[gg % PIPE_TILES], m_refs[gg % PIPE_TILES])
        ahead = gg + 2
        slot = ahead % PIPE_TILES
        if ahead < n_groups:
            scores(ahead // NA_GROUPS, ahead % NA_GROUPS, q_ref, k_ref, kc_ref, bias_ref, s_refs[slot], m_refs[slot])
        else:
            scores(0, ahead - n_groups, qn_ref, kn_ref, kcn_ref, biasn_ref, s_refs[slot], m_refs[slot])


def _na_attn(q, k, v, kc, vc, bias):
    bn, s, w = q.shape
    assert s == NA_ROWS * GRID_W and NA_GROUPS % PIPE_TILES == 0
    pairs = w // LANES
    nb = bn // NA_STEP_BATCH
    assert bn % NA_STEP_BATCH == 0
    last = pairs * nb - 1

    def nxt(p, b):
        u = jnp.minimum(p * nb + b + 1, last)
        return (u % nb) * NA_STEP_BATCH, 0, u // nb

    cur = lambda p, b: (b, 0, p)
    spec = lambda a, width: pl.BlockSpec((NA_STEP_BATCH, a.shape[1], width), cur)
    spec_next = lambda a, width: pl.BlockSpec((1, a.shape[1], width), nxt)
    rows = 2 * NA_GROUP * GRID_W
    nk = NA_WIN_ROWS * GRID_W + kc.shape[1]
    return pl.pallas_call(
        _na_kernel,
        grid=(pairs, nb),
        in_specs=[spec(q, LANES), spec(k, LANES), spec(v, 2 * LANES),
                  spec(kc, LANES), spec(vc, 2 * LANES),
                  pl.BlockSpec((1,) + bias.shape[1:], lambda p, b: (p, 0, 0, 0, 0)),
                  spec_next(q, LANES), spec_next(k, LANES), spec_next(kc, LANES),
                  pl.BlockSpec((1,) + bias.shape[1:], lambda p, b: (nxt(p, b)[2], 0, 0, 0, 0))],
        out_specs=spec(q, LANES),
        out_shape=jax.ShapeDtypeStruct((bn, s, w), BF16),
        scratch_shapes=[pltpu.VMEM((rows, nk), F32)] * PIPE_TILES + [pltpu.VMEM((rows, LANES), F32)] * PIPE_TILES,
        compiler_params=pltpu.CompilerParams(
            dimension_semantics=("arbitrary", "arbitrary"), vmem_limit_bytes=VMEM_LIMIT),
        name="na_attn",
    )(q, k, v, kc, vc, bias, q, k, kc, bias)


def _group_matrix(width, group):
    idx = np.arange(width) // group
    return jnp.asarray((idx[:, None] == idx[None, :]).astype(np.float32), dtype=BF16)


def _rope_tables(s):
    t = np.arange(s)
    row = (t // GRID_W).astype(np.float32)
    col = (t % GRID_W).astype(np.float32)
    half = DA_DK // 2
    inv = (np.float32(ROPE_THETA) ** (-np.arange(0, half, 2, dtype=np.float32) / np.float32(half))).astype(np.float32)
    ar = row[:, None] * inv
    ac = col[:, None] * inv
    cr, sr, cc, sc = np.cos(ar), np.sin(ar), np.cos(ac), np.sin(ac)
    cos64 = np.concatenate([cr, cr, cc, cc], axis=1)
    sin64 = np.concatenate([-sr, sr, -sc, sc], axis=1)
    return (jnp.asarray(np.concatenate([cos64, cos64], axis=1), F32),
            jnp.asarray(np.concatenate([sin64, sin64], axis=1), F32))


def _na_bias_table(rpb):
    cols = np.arange(GRID_W)
    col_start = np.clip(cols - NA_KW // 2, 0, GRID_W - NA_KW)
    col_ok = (cols[None, :] >= col_start[:, None]) & (cols[None, :] < col_start[:, None] + NA_KW)
    dc_idx = np.clip(cols[None, :] - cols[:, None], -(NA_KW - 1), NA_KW - 1) + NA_KW - 1
    nrel = 2 * NA_KW - 1
    rpb2 = jnp.concatenate([rpb[:, :-1], rpb[:, 1:]], axis=-1)
    idx2 = np.concatenate([dc_idx, dc_idx + nrel], axis=1)
    ok2 = np.concatenate([col_ok, col_ok], axis=1)
    pairs = jnp.where(jnp.asarray(ok2)[None, None], rpb2[:, :, idx2] * LOG2E, NEG_INF)
    return pairs.reshape(NA_HEADS // 2, 2, 2 * NA_KH - 2, GRID_W, 2 * GRID_W)


def _tile_lanes(g, n):
    return jnp.tile(g, n).reshape(1, -1)


def kernel(x, c, ctx, c_ctx, mod_w, mod_b, norm_mix_g, norm_ffn_g, w_out, ffn_w_in, ffn_w_out, ev_w_in, da_q_g, da_k_g, da_lq1, da_lk1, da_lq2, da_lk2, da_out_g, mla_q_a_g, mla_w_uq, mla_kv_a_g, mla_w_ukv, mla_q_g, mla_k_g, mla_kr_g, od_w_in, na_q_g, na_k_g, na_rpb):
    bsz, seq, d = x.shape
    tm = 256
    tm_x = 2 * PROJ_ROWS
    tq = 256

    cond = jnp.concatenate([c, c_ctx[None], jnp.zeros((7, d), F32)], axis=0)
    mods = _adaln(cond, mod_w, mod_b)

    def mod_vectors(l, row):
        return [(l, row, j) for j in range(6)]

    g64 = _group_matrix(MXU_DIM, 64)
    cos, sin = _rope_tables(seq)

    l = 0
    w_in = jnp.concatenate([ev_w_in[0], jnp.zeros((d, 64), F32)], axis=1).astype(BF16)
    wuq = mla_w_uq[0].reshape(MLA_Q_LORA, MLA_HEADS, MLA_NOPE + MLA_ROPE)
    wuq = jnp.pad(wuq, ((0, 0), (0, 0), (0, 64))).reshape(MLA_Q_LORA, MLA_HEADS * 256).astype(BF16)
    mqg = jnp.tile(jnp.pad(mla_q_g[0], (0, 64)), MLA_HEADS).reshape(1, -1)
    consts = [
        w_in, g64, _group_matrix(LANES, LANES), _group_matrix(MXU_DIM, MXU_DIM),
        _tile_lanes(da_q_g[0], 8), _tile_lanes(da_k_g[0], 8),
        mla_q_a_g[0].reshape(1, -1), wuq, mla_kv_a_g[0].reshape(1, -1), mla_w_ukv[0].astype(BF16),
        mqg, mla_k_g[0].reshape(1, -1), jnp.pad(mla_kr_g[0], (0, 64)).reshape(1, -1),
    ]
    lqk = jnp.stack([da_lq1[0], da_lk1[0], da_lq2[0], da_lk2[0]])
    lam_init = 0.8 - 0.6 * math.exp(-0.3 * l)
    out_g = da_out_g[0].reshape(1, -1)
    w_o = w_out[l].astype(BF16)
    f_in = ffn_w_in[l].astype(BF16)
    f_out = ffn_w_out[l].astype(BF16)
    gm = norm_mix_g[l].reshape(1, d)
    gf = norm_ffn_g[l].reshape(1, d)

    sh_a, sc_a, g_a, sh_f, sc_f, g_f = mod_vectors(l, None)
    csh_a, csc_a, cg_a, csh_f, csc_f, cg_f = mod_vectors(l, bsz)

    px = _proj_even(x, mods, sh_a, sc_a, gm, consts, cos, sin, True, tm_x)
    pc = _proj_even(ctx, mods, csh_a, csc_a, gm, consts, cos[:ctx.shape[1]], sin[:ctx.shape[1]], False, tm)
    dq_x, dk_x, dv_x, mq_x, mk_x, mv_x = px
    dq_c, dk_c, dv_c, mq_c, mk_c, mv_c = pc

    da_x = _attn_pipe(dq_x, dk_c, dk_x, dv_c, dv_x, tq, (lqk, out_g, lam_init))
    mla_x = _attn_pipe(mq_x, mk_c, mk_x, mv_c, mv_x, tq)
    da_c = _ctx_diff_attn(dq_c, dk_c, dv_c, lqk, out_g, lam_init)
    mla_c = _ctx_mla_attn(mq_c, mk_c, mv_c)

    x = _post(x, [da_x, mla_x], mods, g_a, sh_f, sc_f, g_f, gf, w_o, f_in, f_out, POST_STEP_TILES * POST_ROWS)
    ctx = _post(ctx, [da_c, mla_c], mods, cg_a, csh_f, csc_f, cg_f, gf, w_o, f_in, f_out, tm)

    l = 1
    sh_a, sc_a, g_a, sh_f, sc_f, g_f = mod_vectors(l, None)
    csh_a, csc_a = mod_vectors(l, bsz)[:2]
    gm = norm_mix_g[l].reshape(1, d)
    gf = norm_ffn_g[l].reshape(1, d)
    consts = [od_w_in[0].astype(BF16), g64, _tile_lanes(na_q_g[0], NA_HEADS), _tile_lanes(na_k_g[0], NA_HEADS)]
    q_x, k_x, v_x = _proj_odd(x, mods, sh_a, sc_a, gm, consts, tm_x)
    _, k_c, v_c = _proj_odd(ctx, mods, csh_a, csc_a, gm, consts, tm)
    y = _na_attn(q_x, k_x, v_x, k_c, v_c, _na_bias_table(na_rpb[0]))
    x = _post(x, [y], mods, g_a, sh_f, sc_f, g_f, gf, w_out[l].astype(BF16),
              ffn_w_in[l].astype(BF16), ffn_w_out[l].astype(BF16), POST_STEP_TILES * POST_ROWS)
    return x
```

```python
import functools
import math

import jax
import jax.numpy as jnp
import numpy as np
from jax import lax
from jax.experimental import pallas as pl
from jax.experimental.pallas import tpu as pltpu

D_MODEL = 1024
DEPTH = 2
GRID_W = 64
DA_HEADS = 4
DA_DK = 64
DA_DV = 2 * DA_DK
MLA_HEADS = 4
MLA_NOPE = 128
MLA_ROPE = 64
MLA_V = 128
MLA_Q_LORA = 256
MLA_KV_LORA = 128
NA_HEADS = 16
NA_DH = 64
NA_KH = 8
NA_KW = 16
D_FF = -(-8 * D_MODEL // (3 * 256)) * 256
ROPE_THETA = 10000.0
EPS = 1e-6
NEG_INF = -1e30
LOG2E = math.log2(math.e)
SEQ = 2048
NA_ROWS = SEQ // GRID_W
NA_GROUP = 4
NA_GROUPS = NA_ROWS // NA_GROUP
NA_STEP_BATCH = 2
NA_WIN_ROWS = NA_GROUP + NA_KH

LANES = 128
MXU_DIM = 256
VMEM_LIMIT = 56 * 1024 * 1024

PROJ_ROWS = 256
POST_STEP_TILES = 4
POST_ROWS = 256
PIPE_TILES = 4
ATTN_STEP_TILES = 8

BF16 = jnp.bfloat16
F32 = jnp.float32


def _dot(a, b):
    return jnp.dot(a, b, preferred_element_type=F32)


def _dot_nt(a, b):
    return lax.dot_general(a, b, (((1,), (1,)), ((), ())), preferred_element_type=F32)


def _group_rsqrt(x, gmat, size):
    x2 = (x * x).astype(BF16)
    w = gmat.shape[0]
    n = x2.shape[1] // w
    sums = [_dot(x2[:, i * w:(i + 1) * w], gmat) for i in range(n)]
    ss = sums[0] if n == 1 else jnp.concatenate(sums, axis=1)
    return lax.rsqrt(ss * (1.0 / size) + EPS)


def _swap16(x):
    n = x.shape[1]
    lane = lax.broadcasted_iota(jnp.int32, (1, n), 1)
    up = pltpu.roll(x, n - 16, 1)
    down = pltpu.roll(x, 16, 1)
    return jnp.where((lane % 32) < 16, up, down)


def _rope(x, cos, sin_signed):
    return x * cos + _swap16(x) * sin_signed


def _modulated_norm(x, g, shift, scale):
    y = x * lax.rsqrt(jnp.mean(x * x, axis=-1, keepdims=True) + EPS)
    return (y * g) * (1.0 + scale) + shift


def _adaln_kernel(cond_ref, w_ref, b_ref, o_ref):
    c = cond_ref[...]
    a = (c * jax.nn.sigmoid(c)).astype(BF16)
    res = _dot(a, w_ref[0].astype(BF16)) + b_ref[0]
    for r in range(res.shape[0]):
        o_ref[0, r] = res[r:r + 1, :]


def _adaln(cond, mod_w, mod_b):
    rows = cond.shape[0]
    n = mod_w.shape[2]
    tn = 1536
    return pl.pallas_call(
        _adaln_kernel,
        grid=(DEPTH, n // tn),
        in_specs=[
            pl.BlockSpec((rows, D_MODEL), lambda l, j: (0, 0)),
            pl.BlockSpec((1, D_MODEL, tn), lambda l, j: (l, 0, j)),
            pl.BlockSpec((1, 1, tn), lambda l, j: (l, 0, j)),
        ],
        out_specs=pl.BlockSpec((1, rows, 1, tn), lambda l, j: (l, 0, 0, j)),
        out_shape=jax.ShapeDtypeStruct((DEPTH, rows, 1, n), F32),
        compiler_params=pltpu.CompilerParams(
            dimension_semantics=("arbitrary", "arbitrary"), vmem_limit_bytes=VMEM_LIMIT),
        name="adaln",
    )(cond, mod_w, mod_b.reshape(DEPTH, 1, n))


def _proj_even_kernel(rope, *refs):
    for t in range(refs[0].shape[1] // PROJ_ROWS):
        _proj_even_tile(rope, slice(t * PROJ_ROWS, (t + 1) * PROJ_ROWS), *refs)


def _proj_even_tile(rope, rs, x_ref, sh_ref, sc_ref, g_ref, w_ref, g64_ref, g128_ref, g256_ref,
                    dqg_ref, dkg_ref, qag_ref, wuq_ref, kvag_ref, wukv_ref, mqg_ref, mkg_ref,
                    krg_ref, cos_ref, sin_ref,
                    dq_ref, dk_ref, dv_ref, mq_ref, mk_ref, mv_ref):
    h = _modulated_norm(x_ref[0, rs, :], g_ref[...], sh_ref[0, 0], sc_ref[0, 0]).astype(BF16)
    p = _dot(h, w_ref[...])
    n_dq = DA_HEADS * 2 * DA_DK
    if rope:
        cos128 = cos_ref[rs, :]
        sin128 = sin_ref[rs, :]
        cos512 = jnp.concatenate([cos128] * 4, axis=1)
        sin512 = jnp.concatenate([sin128] * 4, axis=1)
        lane = lax.broadcasted_iota(jnp.int32, (1, LANES), 1)
        cos_half = jnp.where(lane < MLA_ROPE, cos128, 1.0)
        sin_half = jnp.where(lane < MLA_ROPE, sin128, 0.0)

    def da_qk(v, gain, scale):
        y = v * _group_rsqrt(v, g64_ref[...], DA_DK) * (gain * scale)
        if rope:
            y = _rope(y, cos512, sin512)
        return y.astype(BF16)

    dq_ref[0, rs, :] = da_qk(p[:, 0:n_dq], dqg_ref[...], DA_DK ** -0.5 * LOG2E)
    dk_ref[0, rs, :] = da_qk(p[:, n_dq:2 * n_dq], dkg_ref[...], 1.0)
    ones = jnp.ones((p.shape[0], LANES), BF16)
    dv = p[:, 2 * n_dq:3 * n_dq].astype(BF16)
    dv_ref[0, rs, :] = jnp.concatenate(
        [blk for hh in range(DA_HEADS) for blk in (dv[:, hh * DA_DV:(hh + 1) * DA_DV], ones)], axis=1)

    c0 = 3 * n_dq
    cq = p[:, c0:c0 + MLA_Q_LORA]
    cq = cq * _group_rsqrt(cq, g256_ref[...], MLA_Q_LORA) * qag_ref[...]
    q = _dot(cq.astype(BF16), wuq_ref[...])
    q = q * _group_rsqrt(q, g256_ref[...], MLA_NOPE + MLA_ROPE) * (
        mqg_ref[...] * ((MLA_NOPE + MLA_ROPE) ** -0.5 * LOG2E))

    c1 = c0 + MLA_Q_LORA
    ckv = p[:, c1:c1 + MLA_KV_LORA]
    ckv = ckv * _group_rsqrt(ckv, g128_ref[...], MLA_KV_LORA) * kvag_ref[...]
    kv = _dot(ckv.astype(BF16), wukv_ref[...])

    c2 = c1 + MLA_KV_LORA
    kr = p[:, c2:c2 + LANES]
    kr = kr * _group_rsqrt(kr, g128_ref[...], MLA_ROPE) * krg_ref[...]
    if rope:
        kr = _rope(kr, cos_half, sin_half)
    kr = kr.astype(BF16)

    mq, mk, mv = [], [], []
    for hh in range(MLA_HEADS):
        b0 = hh * 2 * LANES
        qn = q[:, b0:b0 + LANES]
        qr = q[:, b0 + LANES:b0 + 2 * LANES]
        if rope:
            qr = _rope(qr, cos_half, sin_half)
        mq += [qn.astype(BF16), qr.astype(BF16)]
        kn = kv[:, b0:b0 + LANES]
        kn = kn * _group_rsqrt(kn, g128_ref[...], MLA_NOPE) * mkg_ref[...]
        mk += [kn.astype(BF16), kr]
        mv += [kv[:, b0 + LANES:b0 + 2 * LANES].astype(BF16), ones]
    mq_ref[0, rs, :] = jnp.concatenate(mq, axis=1)
    mk_ref[0, rs, :] = jnp.concatenate(mk, axis=1)
    mv_ref[0, rs, :] = jnp.concatenate(mv, axis=1)


def _const_spec(a):
    nd = a.ndim
    return pl.BlockSpec(a.shape, lambda b, i: (0,) * nd)


def _mod_spec(vec):
    l, row, j = vec
    return pl.BlockSpec((1, 1, 1, D_MODEL), lambda b, i: (l, b if row is None else row, 0, j))


def _proj_even(x, mods, shift, scale, gain, consts, cos, sin, rope, tm):
    bn, t, d = x.shape
    tok = lambda w: pl.BlockSpec((1, tm, w), lambda b, i: (b, i, 0))
    rope_spec = pl.BlockSpec((tm, LANES), lambda b, i: (i, 0))
    widths = (512, 512, 1024, 1024, 1024, 1024)
    return pl.pallas_call(
        functools.partial(_proj_even_kernel, rope),
        grid=(bn, t // tm),
        in_specs=[tok(d), _mod_spec(shift), _mod_spec(scale), _const_spec(gain)]
                 + [_const_spec(c) for c in consts] + [rope_spec, rope_spec],
        out_specs=[tok(w) for w in widths],
        out_shape=[jax.ShapeDtypeStruct((bn, t, w), BF16) for w in widths],
        compiler_params=pltpu.CompilerParams(
            dimension_semantics=("arbitrary", "arbitrary"), vmem_limit_bytes=VMEM_LIMIT),
        name="proj_even",
    )(x, mods, mods, gain, *consts, cos, sin)


def _softmax_parts(s_list):
    m = functools.reduce(jnp.maximum, [jnp.max(s, axis=-1, keepdims=True) for s in s_list])
    e_list = [jnp.exp2(s - m) for s in s_list]
    l = functools.reduce(jnp.add, [jnp.sum(e, axis=-1, keepdims=True) for e in e_list])
    return e_list, l


def _ctx_diff_attn_kernel(lam_init, q_ref, k_ref, v_ref, lqk_ref, og_ref, o_ref):
    lqk = lqk_ref[...]
    lam = (jnp.exp(jnp.sum(lqk[0:1] * lqk[1:2], axis=-1, keepdims=True))
           - jnp.exp(jnp.sum(lqk[2:3] * lqk[3:4], axis=-1, keepdims=True)) + lam_init)
    lane = lax.broadcasted_iota(jnp.int32, (1, LANES), 1)
    for h in range(DA_HEADS):
        hs = slice(h * LANES, (h + 1) * LANES)
        q = q_ref[0, :, hs]
        k = k_ref[0, :, hs]
        v = v_ref[0, :, 2 * h * LANES:(2 * h + 1) * LANES]
        zero = jnp.zeros_like(q)
        (e1,), l1 = _softmax_parts([_dot_nt(jnp.where(lane < DA_DK, q, zero), k)])
        (e2,), l2 = _softmax_parts([_dot_nt(jnp.where(lane < DA_DK, zero, q), k)])
        o = _dot((e1 * (1.0 / l1) - e2 * (lam / l2)).astype(BF16), v)
        o = o * lax.rsqrt(jnp.mean(o * o, axis=-1, keepdims=True) + EPS) * (og_ref[...] * (1.0 - lam_init))
        o_ref[0, :, hs] = o.astype(BF16)


def _ctx_diff_attn(q, k, v, lqk, out_g, lam_init):
    bn, n, _ = q.shape
    spec = lambda a: pl.BlockSpec((1, n, a.shape[2]), lambda b: (b, 0, 0))
    cspec = lambda a: pl.BlockSpec(a.shape, lambda b: (0, 0))
    return pl.pallas_call(
        functools.partial(_ctx_diff_attn_kernel, lam_init),
        grid=(bn,),
        in_specs=[spec(q), spec(k), spec(v), cspec(lqk), cspec(out_g)],
        out_specs=spec(q),
        out_shape=jax.ShapeDtypeStruct((bn, n, DA_HEADS * DA_DV), BF16),
        compiler_params=pltpu.CompilerParams(dimension_semantics=("arbitrary",), vmem_limit_bytes=VMEM_LIMIT),
        name="ctx_diff_attn",
    )(q, k, v, lqk, out_g)


def _ctx_mla_attn_kernel(q_ref, k_ref, v_ref, o_ref):
    hw = 2 * LANES
    for h in range(MLA_HEADS):
        q = q_ref[0, :, h * hw:(h + 1) * hw]
        k = k_ref[0, :, h * hw:(h + 1) * hw]
        v = v_ref[0, :, h * hw:h * hw + LANES]
        (e,), l = _softmax_parts([_dot_nt(q, k)])
        o_ref[0, :, h * LANES:(h + 1) * LANES] = (_dot(e.astype(BF16), v) * (1.0 / l)).astype(BF16)


def _ctx_mla_attn(q, k, v):
    bn, n, _ = q.shape
    spec = lambda a: pl.BlockSpec((1, n, a.shape[2]), lambda b: (b, 0, 0))
    return pl.pallas_call(
        _ctx_mla_attn_kernel,
        grid=(bn,),
        in_specs=[spec(q), spec(k), spec(v)],
        out_specs=pl.BlockSpec((1, n, MLA_HEADS * MLA_V), lambda b: (b, 0, 0)),
        out_shape=jax.ShapeDtypeStruct((bn, n, MLA_HEADS * MLA_V), BF16),
        compiler_params=pltpu.CompilerParams(dimension_semantics=("arbitrary",), vmem_limit_bytes=VMEM_LIMIT),
        name="ctx_mla_attn",
    )(q, k, v)


def _next_step_maps(nb, nh, ns):
    last = nb * nh * ns - 1

    def nxt(b, h, i):
        u = jnp.minimum((b * nh + h) * ns + i + 1, last)
        return u // (nh * ns), (u // ns) % nh, u % ns

    return nxt


def _is_first_step():
    return (pl.program_id(0) == 0) & (pl.program_id(1) == 0) & (pl.program_id(2) == 0)


def _lane_bcast(col, rows):
    return jnp.broadcast_to(col, (rows, LANES))


def _attn_pipe_kernel(diff, lam_init, *refs):
    n_in = 10 if diff else 8
    q_ref, qn_ref, kc_ref, kx_ref, kcn_ref, kxn_ref, vc_ref, vx_ref = refs[:8]
    o_ref = refs[n_in]
    s_refs = refs[n_in + 1:n_in + 1 + PIPE_TILES]
    m_refs = refs[n_in + 1 + PIPE_TILES:]
    rows = s_refs[0].shape[0]
    tq = rows // 2 if diff else rows
    nc = kc_ref.shape[1]
    nk = s_refs[0].shape[1]
    lane = lax.broadcasted_iota(jnp.int32, (1, LANES), 1)

    def scores(q, kc, kx, s_ref, m_ref):
        if diff:
            zero = jnp.zeros_like(q)
            q = jnp.concatenate([jnp.where(lane < DA_DK, q, zero), jnp.where(lane < DA_DK, zero, q)], axis=0)
        sc = _dot_nt(q, kc[0])
        sx = _dot_nt(q, kx[0])
        s_ref[:, :nc] = sc
        s_ref[:, nc:] = sx
        m = jnp.maximum(jnp.max(sc, axis=-1, keepdims=True), jnp.max(sx, axis=-1, keepdims=True))
        m_ref[...] = _lane_bcast(m, rows)

    def attend(s_ref, m_ref, out_rows):
        m = jnp.concatenate([m_ref[...]] * 2, axis=1)
        acc = None
        for j in range(nk // MXU_DIM):
            k0 = j * MXU_DIM
            p = jnp.exp2(s_ref[:, k0:k0 + MXU_DIM] - m).astype(BF16)
            v = vc_ref[0, k0:k0 + MXU_DIM, :] if k0 < nc else vx_ref[0, k0 - nc:k0 - nc + MXU_DIM, :]
            part = _dot(p, v)
            acc = part if acc is None else acc + part
        o = acc[:, :LANES] * (1.0 / acc[:, LANES:])
        if diff:
            lqk_ref, og_ref = refs[8:10]
            lqk = lqk_ref[...]
            lam = (jnp.exp(jnp.sum(lqk[0:1] * lqk[1:2], axis=-1, keepdims=True))
                   - jnp.exp(jnp.sum(lqk[2:3] * lqk[3:4], axis=-1, keepdims=True)) + lam_init)
            o = o[:tq] - lam * o[tq:]
            o = o * lax.rsqrt(jnp.mean(o * o, axis=-1, keepdims=True) + EPS) * (og_ref[...] * (1.0 - lam_init))
        o_ref[0, out_rows, :] = o.astype(BF16)

    def tile_rows(t):
        return slice(t * tq, (t + 1) * tq)

    @pl.when(_is_first_step())
    def _():
        for t in range(2):
            scores(q_ref[0, tile_rows(t), :], kc_ref, kx_ref, s_refs[t], m_refs[t])

    n_tiles = q_ref.shape[1] // tq
    for t in range(n_tiles):
        attend(s_refs[t % PIPE_TILES], m_refs[t % PIPE_TILES], tile_rows(t))
        ahead = t + 2
        slot = ahead % PIPE_TILES
        if ahead < n_tiles:
            scores(q_ref[0, tile_rows(ahead), :], kc_ref, kx_ref, s_refs[slot], m_refs[slot])
        else:
            scores(qn_ref[0, tile_rows(ahead - n_tiles), :], kcn_ref, kxn_ref, s_refs[slot], m_refs[slot])


def _attn_pipe(q, kc, kx, vc, vx, tq, diff_params=None):
    diff = diff_params is not None
    bn, t, _ = q.shape
    nh = DA_HEADS if diff else MLA_HEADS
    hw = LANES if diff else 2 * LANES
    step_rows = min(ATTN_STEP_TILES * tq, t)
    ns = t // step_rows
    assert t % step_rows == 0 and (step_rows // tq) % PIPE_TILES == 0
    rows = 2 * tq if diff else tq
    nkeys = kc.shape[1] + kx.shape[1]
    nxt = _next_step_maps(bn, nh, ns)

    def q_next(b, h, i):
        b2, h2, i2 = nxt(b, h, i)
        return b2, i2, h2

    def k_next(b, h, i):
        b2, h2, _ = nxt(b, h, i)
        return b2, 0, h2

    cur = lambda b, h, i: (b, 0, h)
    cspec = lambda a: pl.BlockSpec(a.shape, lambda b, h, i: (0, 0))
    extra = list(diff_params[:2]) if diff else []
    lam_init = diff_params[2] if diff else 0.0
    return pl.pallas_call(
        functools.partial(_attn_pipe_kernel, diff, lam_init),
        grid=(bn, nh, ns),
        in_specs=[pl.BlockSpec((1, step_rows, hw), lambda b, h, i: (b, i, h)),
                  pl.BlockSpec((1, step_rows, hw), q_next),
                  pl.BlockSpec((1, kc.shape[1], hw), cur),
                  pl.BlockSpec((1, kx.shape[1], hw), cur),
                  pl.BlockSpec((1, kc.shape[1], hw), k_next),
                  pl.BlockSpec((1, kx.shape[1], hw), k_next),
                  pl.BlockSpec((1, vc.shape[1], 2 * LANES), cur),
                  pl.BlockSpec((1, vx.shape[1], 2 * LANES), cur)]
                 + [cspec(a) for a in extra],
        out_specs=pl.BlockSpec((1, step_rows, LANES), lambda b, h, i: (b, i, h)),
        out_shape=jax.ShapeDtypeStruct((bn, t, nh * LANES), BF16),
        scratch_shapes=[pltpu.VMEM((rows, nkeys), F32)] * PIPE_TILES + [pltpu.VMEM((rows, LANES), F32)] * PIPE_TILES,
        compiler_params=pltpu.CompilerParams(
            dimension_semantics=("arbitrary",) * 3, vmem_limit_bytes=VMEM_LIMIT),
        name="diff_pipe" if diff else "mla_pipe",
    )(q, q, kc, kx, kc, kx, vc, vx, *extra)


def _post_kernel(ny, *refs):
    x_ref = refs[0]
    y_refs = refs[1:1 + ny]
    ga_ref, shf_ref, scf_ref, gf_ref, g_ref, wo_ref = refs[1 + ny:7 + ny]
    win_ref, wout_ref, o_ref, x1_a, x1_b, h_a, h_b = refs[7 + ny:]
    x1_bufs, h_bufs = (x1_a, x1_b), (h_a, h_b)
    nsub = x_ref.shape[1] // POST_ROWS
    w_rows = np.cumsum([0] + [y.shape[2] for y in y_refs])

    def mix(t):
        rs = slice(t * POST_ROWS, (t + 1) * POST_ROWS)
        attn = functools.reduce(jnp.add, [_dot(y[0, rs, :], wo_ref[int(w_rows[j]):int(w_rows[j + 1]), :])
                                          for j, y in enumerate(y_refs)])
        x1 = x_ref[0, rs, :] + ga_ref[0, 0] * attn
        x1_bufs[t % 2][...] = x1
        h_bufs[t % 2][...] = _modulated_norm(x1, g_ref[...], shf_ref[0, 0], scf_ref[0, 0]).astype(BF16)

    def ffn(t):
        rs = slice(t * POST_ROWS, (t + 1) * POST_ROWS)
        u = _dot(h_bufs[t % 2][...], win_ref[...])
        gate = u[:, :D_FF]
        act = (gate * jax.nn.sigmoid(gate) * u[:, D_FF:]).astype(BF16)
        o_ref[0, rs, :] = x1_bufs[t % 2][...] + gf_ref[0, 0] * _dot(act, wout_ref[...])

    mix(0)
    for t in range(nsub):
        if t + 1 < nsub:
            mix(t + 1)
        ffn(t)


def _post(x, ys, mods, gate_a, shift_f, scale_f, gate_f, gain, layer, w_o, w_in, w_out, tm):
    bn, t, d = x.shape
    tok = lambda w: pl.BlockSpec((1, tm, w), lambda b, i: (b, i, 0))
    wspec = lambda a: pl.BlockSpec((None,) + a.shape[1:], lambda b, i: (layer, 0, 0), pipeline_mode=pl.Buffered(1))
    return pl.pallas_call(
        functools.partial(_post_kernel, len(ys)),
        grid=(bn, t // tm),
        in_specs=[tok(d)] + [tok(y.shape[2]) for y in ys]
                 + [_mod_spec(gate_a), _mod_spec(shift_f), _mod_spec(scale_f), _mod_spec(gate_f), _const_spec(gain)]
                 + [wspec(w_o), wspec(w_in), wspec(w_out)],
        out_specs=tok(d),
        out_shape=jax.ShapeDtypeStruct((bn, t, d), F32),
        scratch_shapes=[pltpu.VMEM((POST_ROWS, d), F32)] * 2 + [pltpu.VMEM((POST_ROWS, d), BF16)] * 2,
        compiler_params=pltpu.CompilerParams(
            dimension_semantics=("arbitrary", "arbitrary"), vmem_limit_bytes=VMEM_LIMIT),
        name="post",
    )(x, *ys, mods, mods, mods, mods, gain, w_o, w_in, w_out)


def _proj_odd_kernel(*refs):
    for t in range(refs[0].shape[1] // PROJ_ROWS):
        _proj_odd_tile(slice(t * PROJ_ROWS, (t + 1) * PROJ_ROWS), *refs)


def _proj_odd_tile(rs, x_ref, sh_ref, sc_ref, g_ref, w_ref, g64_ref, qg_ref, kg_ref, q_ref, k_ref, v_ref):
    h = _modulated_norm(x_ref[0, rs, :], g_ref[...], sh_ref[0, 0], sc_ref[0, 0]).astype(BF16)
    p = _dot(h, w_ref[...])
    w = NA_HEADS * NA_DH

    def qk(v, gain, scale):
        return (v * _group_rsqrt(v, g64_ref[...], NA_DH) * (gain * scale)).astype(BF16)

    q_ref[0, rs, :] = qk(p[:, :w], qg_ref[...], NA_DH ** -0.5 * LOG2E)
    k_ref[0, rs, :] = qk(p[:, w:2 * w], kg_ref[...], 1.0)
    v = p[:, 2 * w:].astype(BF16)
    ones = jnp.ones((p.shape[0], LANES), BF16)
    v_ref[0, rs, :] = jnp.concatenate(
        [blk for j in range(w // LANES) for blk in (v[:, j * LANES:(j + 1) * LANES], ones)], axis=1)


def _proj_odd(x, mods, shift, scale, gain, consts, tm):
    bn, t, d = x.shape
    tok = lambda w: pl.BlockSpec((1, tm, w), lambda b, i: (b, i, 0))
    w = NA_HEADS * NA_DH
    widths = (w, w, 2 * w)
    return pl.pallas_call(
        _proj_odd_kernel,
        grid=(bn, t // tm),
        in_specs=[tok(d), _mod_spec(shift), _mod_spec(scale), _const_spec(gain)] + [_const_spec(c) for c in consts],
        out_specs=[tok(n) for n in widths],
        out_shape=[jax.ShapeDtypeStruct((bn, t, n), BF16) for n in widths],
        compiler_params=pltpu.CompilerParams(
            dimension_semantics=("arbitrary", "arbitrary"), vmem_limit_bytes=VMEM_LIMIT),
        name="proj_odd",
    )(x, mods, mods, gain, *consts)


def _na_window_start(g):
    return min(max(NA_GROUP * g - NA_KH // 2, 0), NA_ROWS - NA_WIN_ROWS)


def _na_bias_tile(g, bias_r, lo):
    ws = _na_window_start(g)
    neg = jnp.full((GRID_W, LANES), NEG_INF, F32)
    row_tiles = []
    for head in range(2):
        for a in range(NA_GROUP):
            r = NA_GROUP * g + a
            rs = min(max(r - NA_KH // 2, 0), NA_ROWS - NA_KH)
            blocks = []
            for jp in range(NA_WIN_ROWS // 2):
                kr0 = ws + 2 * jp
                ok0 = rs <= kr0 < rs + NA_KH
                ok1 = rs <= kr0 + 1 < rs + NA_KH
                if not (ok0 or ok1):
                    blocks.append(neg)
                    continue
                blk = bias_r[0, head, kr0 - r + NA_KH - 1]
                if not ok0:
                    blk = jnp.where(lo, neg, blk)
                if not ok1:
                    blk = jnp.where(lo, blk, neg)
                blocks.append(blk)
            row_tiles.append(jnp.concatenate(blocks, axis=1))
    return jnp.concatenate(row_tiles, axis=0)


def _na_kernel(q_ref, k_ref, v_ref, kc_ref, vc_ref, bias_ref, qn_ref, kn_ref, kcn_ref, biasn_ref, o_ref,
               *scratch):
    s_refs = scratch[:PIPE_TILES]
    m_refs = scratch[PIPE_TILES:]
    lane = lax.broadcasted_iota(jnp.int32, (1, LANES), 1)
    lo = lane < NA_DH
    gq = NA_GROUP * GRID_W
    nloc = NA_WIN_ROWS * GRID_W
    nk = s_refs[0].shape[1]

    def win(g):
        k0 = _na_window_start(g) * GRID_W
        return slice(k0, k0 + nloc)

    def scores(bi, g, q_r, k_r, kc_r, bias_r, s_ref, m_ref):
        q = q_r[bi, g * gq:(g + 1) * gq, :]
        zero = jnp.zeros_like(q)
        qq = jnp.concatenate([jnp.where(lo, q, zero), jnp.where(lo, zero, q)], axis=0)
        s_loc = _dot_nt(qq, k_r[bi, win(g), :]) + _na_bias_tile(g, bias_r, lo)
        s_ctx = _dot_nt(qq, kc_r[bi])
        s_ref[:, :nloc] = s_loc
        s_ref[:, nloc:] = s_ctx
        m = jnp.maximum(jnp.max(s_loc, axis=-1, keepdims=True), jnp.max(s_ctx, axis=-1, keepdims=True))
        m_ref[...] = _lane_bcast(m, 2 * gq)

    def attend(bi, g, s_ref, m_ref):
        m = jnp.concatenate([m_ref[...]] * 2, axis=1)
        k0 = _na_window_start(g) * GRID_W
        acc = None
        for j in range(nk // MXU_DIM):
            c0 = j * MXU_DIM
            p = jnp.exp2(s_ref[:, c0:c0 + MXU_DIM] - m).astype(BF16)
            v = v_ref[bi, k0 + c0:k0 + c0 + MXU_DIM, :] if c0 < nloc else vc_ref[bi, c0 - nloc:c0 - nloc + MXU_DIM, :]
            part = _dot(p, v)
            acc = part if acc is None else acc + part
        o = acc[:, :LANES] * (1.0 / acc[:, LANES:])
        o_ref[bi, g * gq:(g + 1) * gq, :] = jnp.where(lo, o[:gq], o[gq:]).astype(BF16)

    @pl.when((pl.program_id(0) == 0) & (pl.program_id(1) == 0))
    def _():
        for g in range(2):
            scores(0, g, q_ref, k_ref, kc_ref, bias_ref, s_refs[g], m_refs[g])

    n_groups = q_ref.shape[0] * NA_GROUPS
    for gg in range(n_groups):
        attend(gg // NA_GROUPS, gg % NA_GROUPS, s_refs[gg % PIPE_TILES], m_refs[gg % PIPE_TILES])
        ahead = gg + 2
        slot = ahead % PIPE_TILES
        if ahead < n_groups:
            scores(ahead // NA_GROUPS, ahead % NA_GROUPS, q_ref, k_ref, kc_ref, bias_ref, s_refs[slot], m_refs[slot])
        else:
            scores(0, ahead - n_groups, qn_ref, kn_ref, kcn_ref, biasn_ref, s_refs[slot], m_refs[slot])


def _na_attn(q, k, v, kc, vc, bias):
    bn, s, w = q.shape
    assert s == NA_ROWS * GRID_W and NA_GROUPS % PIPE_TILES == 0
    pairs = w // LANES
    nb = bn // NA_STEP_BATCH
    assert bn % NA_STEP_BATCH == 0
    last = pairs * nb - 1

    def nxt(p, b):
        u = jnp.minimum(p * nb + b + 1, last)
        return (u % nb) * NA_STEP_BATCH, 0, u // nb

    cur = lambda p, b: (b, 0, p)
    spec = lambda a, width: pl.BlockSpec((NA_STEP_BATCH, a.shape[1], width), cur)
    spec_next = lambda a, width: pl.BlockSpec((1, a.shape[1], width), nxt)
    rows = 2 * NA_GROUP * GRID_W
    nk = NA_WIN_ROWS * GRID_W + kc.shape[1]
    return pl.pallas_call(
        _na_kernel,
        grid=(pairs, nb),
        in_specs=[spec(q, LANES), spec(k, LANES), spec(v, 2 * LANES),
                  spec(kc, LANES), spec(vc, 2 * LANES),
                  pl.BlockSpec((1,) + bias.shape[1:], lambda p, b: (p, 0, 0, 0, 0)),
                  spec_next(q, LANES), spec_next(k, LANES), spec_next(kc, LANES),
                  pl.BlockSpec((1,) + bias.shape[1:], lambda p, b: (nxt(p, b)[2], 0, 0, 0, 0))],
        out_specs=spec(q, LANES),
        out_shape=jax.ShapeDtypeStruct((bn, s, w), BF16),
        scratch_shapes=[pltpu.VMEM((rows, nk), F32)] * PIPE_TILES + [pltpu.VMEM((rows, LANES), F32)] * PIPE_TILES,
        compiler_params=pltpu.CompilerParams(
            dimension_semantics=("arbitrary", "arbitrary"), vmem_limit_bytes=VMEM_LIMIT),
        name="na_attn",
    )(q, k, v, kc, vc, bias, q, k, kc, bias)


def _group_matrix(width, group):
    idx = np.arange(width) // group
    return jnp.asarray((idx[:, None] == idx[None, :]).astype(np.float32), dtype=BF16)


def _rope_tables(s):
    t = np.arange(s)
    row = (t // GRID_W).astype(np.float32)
    col = (t % GRID_W).astype(np.float32)
    half = DA_DK // 2
    inv = (np.float32(ROPE_THETA) ** (-np.arange(0, half, 2, dtype=np.float32) / np.float32(half))).astype(np.float32)
    ar = row[:, None] * inv
    ac = col[:, None] * inv
    cr, sr, cc, sc = np.cos(ar), np.sin(ar), np.cos(ac), np.sin(ac)
    cos64 = np.concatenate([cr, cr, cc, cc], axis=1)
    sin64 = np.concatenate([-sr, sr, -sc, sc], axis=1)
    return (jnp.asarray(np.concatenate([cos64, cos64], axis=1), F32),
            jnp.asarray(np.concatenate([sin64, sin64], axis=1), F32))


def _na_bias_table(rpb):
    cols = np.arange(GRID_W)
    col_start = np.clip(cols - NA_KW // 2, 0, GRID_W - NA_KW)
    col_ok = (cols[None, :] >= col_start[:, None]) & (cols[None, :] < col_start[:, None] + NA_KW)
    dc_idx = np.clip(cols[None, :] - cols[:, None], -(NA_KW - 1), NA_KW - 1) + NA_KW - 1
    by_col = jnp.where(jnp.asarray(col_ok)[None, None], rpb[:, :, dc_idx] * LOG2E, NEG_INF)
    pairs = jnp.concatenate([by_col[:, :-1], by_col[:, 1:]], axis=-1)
    return pairs.reshape(NA_HEADS // 2, 2, 2 * NA_KH - 2, GRID_W, 2 * GRID_W)


def _tile_lanes(g, n):
    return jnp.tile(g, n).reshape(1, -1)


def kernel(x, c, ctx, c_ctx, mod_w, mod_b, norm_mix_g, norm_ffn_g, w_out, ffn_w_in, ffn_w_out, ev_w_in, da_q_g, da_k_g, da_lq1, da_lk1, da_lq2, da_lk2, da_out_g, mla_q_a_g, mla_w_uq, mla_kv_a_g, mla_w_ukv, mla_q_g, mla_k_g, mla_kr_g, od_w_in, na_q_g, na_k_g, na_rpb):
    bsz, seq, d = x.shape
    tm = 256
    tm_x = 2 * PROJ_ROWS
    tq = 256

    cond = jnp.concatenate([c, c_ctx[None], jnp.zeros((7, d), F32)], axis=0)
    mods = _adaln(cond, mod_w, mod_b)

    def mod_vectors(l, row):
        return [(l, row, j) for j in range(6)]

    g64 = _group_matrix(MXU_DIM, 64)
    cos, sin = _rope_tables(seq)

    l = 0
    w_in = jnp.concatenate([ev_w_in[0], jnp.zeros((d, 64), F32)], axis=1).astype(BF16)
    wuq = mla_w_uq[0].reshape(MLA_Q_LORA, MLA_HEADS, MLA_NOPE + MLA_ROPE)
    wuq = jnp.pad(wuq, ((0, 0), (0, 0), (0, 64))).reshape(MLA_Q_LORA, MLA_HEADS * 256).astype(BF16)
    mqg = jnp.tile(jnp.pad(mla_q_g[0], (0, 64)), MLA_HEADS).reshape(1, -1)
    consts = [
        w_in, g64, _group_matrix(LANES, LANES), _group_matrix(MXU_DIM, MXU_DIM),
        _tile_lanes(da_q_g[0], 8), _tile_lanes(da_k_g[0], 8),
        mla_q_a_g[0].reshape(1, -1), wuq, mla_kv_a_g[0].reshape(1, -1), mla_w_ukv[0].astype(BF16),
        mqg, mla_k_g[0].reshape(1, -1), jnp.pad(mla_kr_g[0], (0, 64)).reshape(1, -1),
    ]
    lqk = jnp.stack([da_lq1[0], da_lk1[0], da_lq2[0], da_lk2[0]])
    lam_init = 0.8 - 0.6 * math.exp(-0.3 * l)
    out_g = da_out_g[0].reshape(1, -1)
    w_o = w_out.astype(BF16)
    f_in = ffn_w_in.astype(BF16)
    f_out = ffn_w_out.astype(BF16)
    gm = norm_mix_g[l].reshape(1, d)
    gf = norm_ffn_g[l].reshape(1, d)

    sh_a, sc_a, g_a, sh_f, sc_f, g_f = mod_vectors(l, None)
    csh_a, csc_a, cg_a, csh_f, csc_f, cg_f = mod_vectors(l, bsz)

    px = _proj_even(x, mods, sh_a, sc_a, gm, consts, cos, sin, True, tm_x)
    pc = _proj_even(ctx, mods, csh_a, csc_a, gm, consts, cos[:ctx.shape[1]], sin[:ctx.shape[1]], False, tm)
    dq_x, dk_x, dv_x, mq_x, mk_x, mv_x = px
    dq_c, dk_c, dv_c, mq_c, mk_c, mv_c = pc

    da_x = _attn_pipe(dq_x, dk_c, dk_x, dv_c, dv_x, tq, (lqk, out_g, lam_init))
    mla_x = _attn_pipe(mq_x, mk_c, mk_x, mv_c, mv_x, tq)
    da_c = _ctx_diff_attn(dq_c, dk_c, dv_c, lqk, out_g, lam_init)
    mla_c = _ctx_mla_attn(mq_c, mk_c, mv_c)

    x = _post(x, [da_x, mla_x], mods, g_a, sh_f, sc_f, g_f, gf, l, w_o, f_in, f_out, POST_STEP_TILES * POST_ROWS)
    ctx = _post(ctx, [da_c, mla_c], mods, cg_a, csh_f, csc_f, cg_f, gf, l, w_o, f_in, f_out, tm)

    l = 1
    sh_a, sc_a, g_a, sh_f, sc_f, g_f = mod_vectors(l, None)
    csh_a, csc_a = mod_vectors(l, bsz)[:2]
    gm = norm_mix_g[l].reshape(1, d)
    gf = norm_ffn_g[l].reshape(1, d)
    consts = [od_w_in[0].astype(BF16), g64, _tile_lanes(na_q_g[0], NA_HEADS), _tile_lanes(na_k_g[0], NA_HEADS)]
    q_x, k_x, v_x = _proj_odd(x, mods, sh_a, sc_a, gm, consts, tm_x)
    _, k_c, v_c = _proj_odd(ctx, mods, csh_a, csc_a, gm, consts, tm)
    y = _na_attn(q_x, k_x, v_x, k_c, v_c, _na_bias_table(na_rpb[0]))
    x = _post(x, [y], mods, g_a, sh_f, sc_f, g_f, gf, l, w_o, f_in, f_out, POST_STEP_TILES * POST_ROWS)
    return x
```

```python
import functools
import math

import jax
import jax.numpy as jnp
import numpy as np
from jax import lax
from jax.experimental import pallas as pl
from jax.experimental.pallas import tpu as pltpu

D_MODEL = 1024
DEPTH = 2
GRID_W = 64
DA_HEADS = 4
DA_DK = 64
DA_DV = 2 * DA_DK
MLA_HEADS = 4
MLA_NOPE = 128
MLA_ROPE = 64
MLA_V = 128
MLA_Q_LORA = 256
MLA_KV_LORA = 128
NA_HEADS = 16
NA_DH = 64
NA_KH = 8
NA_KW = 16
D_FF = -(-8 * D_MODEL // (3 * 256)) * 256
ROPE_THETA = 10000.0
EPS = 1e-6
NEG_INF = -1e30
LOG2E = math.log2(math.e)
SEQ = 2048
NA_ROWS = SEQ // GRID_W
NA_GROUP = 4
NA_GROUPS = NA_ROWS // NA_GROUP
NA_STEP_BATCH = 2
NA_WIN_ROWS = NA_GROUP + NA_KH

LANES = 128
MXU_DIM = 256
VMEM_LIMIT = 56 * 1024 * 1024

PROJ_ROWS = 256
POST_STEP_TILES = 4
POST_ROWS = 256
PIPE_TILES = 4
ATTN_STEP_TILES = 8

BF16 = jnp.bfloat16
F32 = jnp.float32


def _dot(a, b):
    return jnp.dot(a, b, preferred_element_type=F32)


def _dot_nt(a, b):
    return lax.dot_general(a, b, (((1,), (1,)), ((), ())), preferred_element_type=F32)


def _group_rsqrt(x, gmat, size):
    x2 = (x * x).astype(BF16)
    w = gmat.shape[0]
    n = x2.shape[1] // w
    sums = [_dot(x2[:, i * w:(i + 1) * w], gmat) for i in range(n)]
    ss = sums[0] if n == 1 else jnp.concatenate(sums, axis=1)
    return lax.rsqrt(ss * (1.0 / size) + EPS)


def _swap16(x):
    n = x.shape[1]
    lane = lax.broadcasted_iota(jnp.int32, (1, n), 1)
    up = pltpu.roll(x, n - 16, 1)
    down = pltpu.roll(x, 16, 1)
    return jnp.where((lane % 32) < 16, up, down)


def _rope(x, cos, sin_signed):
    return x * cos + _swap16(x) * sin_signed


def _modulated_norm(x, g, shift, scale):
    y = x * lax.rsqrt(jnp.mean(x * x, axis=-1, keepdims=True) + EPS)
    return (y * g) * (1.0 + scale) + shift


def _adaln_kernel(cond_ref, w_ref, b_ref, o_ref):
    c = cond_ref[...]
    a = (c * jax.nn.sigmoid(c)).astype(BF16)
    res = _dot(a, w_ref[0].astype(BF16)) + b_ref[0]
    for r in range(res.shape[0]):
        o_ref[0, r] = res[r:r + 1, :]


def _adaln(cond, mod_w, mod_b):
    rows = cond.shape[0]
    n = mod_w.shape[2]
    tn = 1536
    return pl.pallas_call(
        _adaln_kernel,
        grid=(DEPTH, n // tn),
        in_specs=[
            pl.BlockSpec((rows, D_MODEL), lambda l, j: (0, 0)),
            pl.BlockSpec((1, D_MODEL, tn), lambda l, j: (l, 0, j)),
            pl.BlockSpec((1, 1, tn), lambda l, j: (l, 0, j)),
        ],
        out_specs=pl.BlockSpec((1, rows, 1, tn), lambda l, j: (l, 0, 0, j)),
        out_shape=jax.ShapeDtypeStruct((DEPTH, rows, 1, n), F32),
        compiler_params=pltpu.CompilerParams(
            dimension_semantics=("arbitrary", "arbitrary"), vmem_limit_bytes=VMEM_LIMIT),
        name="adaln",
    )(cond, mod_w, mod_b.reshape(DEPTH, 1, n))


def _proj_even_kernel(rope, *refs):
    for t in range(refs[0].shape[1] // PROJ_ROWS):
        _proj_even_tile(rope, slice(t * PROJ_ROWS, (t + 1) * PROJ_ROWS), *refs)


def _proj_even_tile(rope, rs, x_ref, sh_ref, sc_ref, g_ref, w_ref, g64_ref, g128_ref, g256_ref,
                    dqg_ref, dkg_ref, qag_ref, wuq_ref, kvag_ref, wukv_ref, mqg_ref, mkg_ref,
                    krg_ref, cos_ref, sin_ref,
                    dq_ref, dk_ref, dv_ref, mq_ref, mk_ref, mv_ref):
    h = _modulated_norm(x_ref[0, rs, :], g_ref[...], sh_ref[0, 0], sc_ref[0, 0]).astype(BF16)
    p = _dot(h, w_ref[...])
    n_dq = DA_HEADS * 2 * DA_DK
    if rope:
        cos128 = cos_ref[rs, :]
        sin128 = sin_ref[rs, :]
        cos512 = jnp.concatenate([cos128] * 4, axis=1)
        sin512 = jnp.concatenate([sin128] * 4, axis=1)
        lane = lax.broadcasted_iota(jnp.int32, (1, LANES), 1)
        cos_half = jnp.where(lane < MLA_ROPE, cos128, 1.0)
        sin_half = jnp.where(lane < MLA_ROPE, sin128, 0.0)

    def da_qk(v, gain, scale):
        y = v * _group_rsqrt(v, g64_ref[...], DA_DK) * (gain * scale)
        if rope:
            y = _rope(y, cos512, sin512)
        return y.astype(BF16)

    dq_ref[0, rs, :] = da_qk(p[:, 0:n_dq], dqg_ref[...], DA_DK ** -0.5 * LOG2E)
    dk_ref[0, rs, :] = da_qk(p[:, n_dq:2 * n_dq], dkg_ref[...], 1.0)
    ones = jnp.ones((p.shape[0], LANES), BF16)
    dv = p[:, 2 * n_dq:3 * n_dq].astype(BF16)
    dv_ref[0, rs, :] = jnp.concatenate(
        [blk for hh in range(DA_HEADS) for blk in (dv[:, hh * DA_DV:(hh + 1) * DA_DV], ones)], axis=1)

    c0 = 3 * n_dq
    cq = p[:, c0:c0 + MLA_Q_LORA]
    cq = cq * _group_rsqrt(cq, g256_ref[...], MLA_Q_LORA) * qag_ref[...]
    q = _dot(cq.astype(BF16), wuq_ref[...])
    q = q * _group_rsqrt(q, g256_ref[...], MLA_NOPE + MLA_ROPE) * (
        mqg_ref[...] * ((MLA_NOPE + MLA_ROPE) ** -0.5 * LOG2E))

    c1 = c0 + MLA_Q_LORA
    ckv = p[:, c1:c1 + MLA_KV_LORA]
    ckv = ckv * _group_rsqrt(ckv, g128_ref[...], MLA_KV_LORA) * kvag_ref[...]
    kv = _dot(ckv.astype(BF16), wukv_ref[...])

    c2 = c1 + MLA_KV_LORA
    kr = p[:, c2:c2 + LANES]
    kr = kr * _group_rsqrt(kr, g128_ref[...], MLA_ROPE) * krg_ref[...]
    if rope:
        kr = _rope(kr, cos_half, sin_half)
    kr = kr.astype(BF16)

    mq, mk, mv = [], [], []
    for hh in range(MLA_HEADS):
        b0 = hh * 2 * LANES
        qn = q[:, b0:b0 + LANES]
        qr = q[:, b0 + LANES:b0 + 2 * LANES]
        if rope:
            qr = _rope(qr, cos_half, sin_half)
        mq += [qn.astype(BF16), qr.astype(BF16)]
        kn = kv[:, b0:b0 + LANES]
        kn = kn * _group_rsqrt(kn, g128_ref[...], MLA_NOPE) * mkg_ref[...]
        mk += [kn.astype(BF16), kr]
        mv += [kv[:, b0 + LANES:b0 + 2 * LANES].astype(BF16), ones]
    mq_ref[0, rs, :] = jnp.concatenate(mq, axis=1)
    mk_ref[0, rs, :] = jnp.concatenate(mk, axis=1)
    mv_ref[0, rs, :] = jnp.concatenate(mv, axis=1)


def _const_spec(a):
    nd = a.ndim
    return pl.BlockSpec(a.shape, lambda b, i: (0,) * nd)


def _mod_spec(vec):
    l, row, j = vec
    return pl.BlockSpec((1, 1, 1, D_MODEL), lambda b, i: (l, b if row is None else row, 0, j))


def _proj_even(x, mods, shift, scale, gain, consts, cos, sin, rope, tm):
    bn, t, d = x.shape
    tok = lambda w: pl.BlockSpec((1, tm, w), lambda b, i: (b, i, 0))
    rope_spec = pl.BlockSpec((tm, LANES), lambda b, i: (i, 0))
    widths = (512, 512, 1024, 1024, 1024, 1024)
    return pl.pallas_call(
        functools.partial(_proj_even_kernel, rope),
        grid=(bn, t // tm),
        in_specs=[tok(d), _mod_spec(shift), _mod_spec(scale), _const_spec(gain)]
                 + [_const_spec(c) for c in consts] + [rope_spec, rope_spec],
        out_specs=[tok(w) for w in widths],
        out_shape=[jax.ShapeDtypeStruct((bn, t, w), BF16) for w in widths],
        compiler_params=pltpu.CompilerParams(
            dimension_semantics=("arbitrary", "arbitrary"), vmem_limit_bytes=VMEM_LIMIT),
        name="proj_even",
    )(x, mods, mods, gain, *consts, cos, sin)


def _softmax_parts(s_list):
    m = functools.reduce(jnp.maximum, [jnp.max(s, axis=-1, keepdims=True) for s in s_list])
    e_list = [jnp.exp2(s - m) for s in s_list]
    l = functools.reduce(jnp.add, [jnp.sum(e, axis=-1, keepdims=True) for e in e_list])
    return e_list, l


def _ctx_diff_attn_kernel(lam_init, q_ref, k_ref, v_ref, lqk_ref, og_ref, o_ref):
    lqk = lqk_ref[...]
    lam = (jnp.exp(jnp.sum(lqk[0:1] * lqk[1:2], axis=-1, keepdims=True))
           - jnp.exp(jnp.sum(lqk[2:3] * lqk[3:4], axis=-1, keepdims=True)) + lam_init)
    lane = lax.broadcasted_iota(jnp.int32, (1, LANES), 1)
    for h in range(DA_HEADS):
        hs = slice(h * LANES, (h + 1) * LANES)
        q = q_ref[0, :, hs]
        k = k_ref[0, :, hs]
        v = v_ref[0, :, 2 * h * LANES:(2 * h + 1) * LANES]
        zero = jnp.zeros_like(q)
        (e1,), l1 = _softmax_parts([_dot_nt(jnp.where(lane < DA_DK, q, zero), k)])
        (e2,), l2 = _softmax_parts([_dot_nt(jnp.where(lane < DA_DK, zero, q), k)])
        o = _dot((e1 * (1.0 / l1) - e2 * (lam / l2)).astype(BF16), v)
        o = o * lax.rsqrt(jnp.mean(o * o, axis=-1, keepdims=True) + EPS) * (og_ref[...] * (1.0 - lam_init))
        o_ref[0, :, hs] = o.astype(BF16)


def _ctx_diff_attn(q, k, v, lqk, out_g, lam_init):
    bn, n, _ = q.shape
    spec = lambda a: pl.BlockSpec((1, n, a.shape[2]), lambda b: (b, 0, 0))
    cspec = lambda a: pl.BlockSpec(a.shape, lambda b: (0, 0))
    return pl.pallas_call(
        functools.partial(_ctx_diff_attn_kernel, lam_init),
        grid=(bn,),
        in_specs=[spec(q), spec(k), spec(v), cspec(lqk), cspec(out_g)],
        out_specs=spec(q),
        out_shape=jax.ShapeDtypeStruct((bn, n, DA_HEADS * DA_DV), BF16),
        compiler_params=pltpu.CompilerParams(dimension_semantics=("arbitrary",), vmem_limit_bytes=VMEM_LIMIT),
        name="ctx_diff_attn",
    )(q, k, v, lqk, out_g)


def _ctx_mla_attn_kernel(q_ref, k_ref, v_ref, o_ref):
    hw = 2 * LANES
    for h in range(MLA_HEADS):
        q = q_ref[0, :, h * hw:(h + 1) * hw]
        k = k_ref[0, :, h * hw:(h + 1) * hw]
        v = v_ref[0, :, h * hw:h * hw + LANES]
        (e,), l = _softmax_parts([_dot_nt(q, k)])
        o_ref[0, :, h * LANES:(h + 1) * LANES] = (_dot(e.astype(BF16), v) * (1.0 / l)).astype(BF16)


def _ctx_mla_attn(q, k, v):
    bn, n, _ = q.shape
    spec = lambda a: pl.BlockSpec((1, n, a.shape[2]), lambda b: (b, 0, 0))
    return pl.pallas_call(
        _ctx_mla_attn_kernel,
        grid=(bn,),
        in_specs=[spec(q), spec(k), spec(v)],
        out_specs=pl.BlockSpec((1, n, MLA_HEADS * MLA_V), lambda b: (b, 0, 0)),
        out_shape=jax.ShapeDtypeStruct((bn, n, MLA_HEADS * MLA_V), BF16),
        compiler_params=pltpu.CompilerParams(dimension_semantics=("arbitrary",), vmem_limit_bytes=VMEM_LIMIT),
        name="ctx_mla_attn",
    )(q, k, v)


def _next_step_maps(nb, nh, ns):
    last = nb * nh * ns - 1

    def nxt(b, h, i):
        u = jnp.minimum((b * nh + h) * ns + i + 1, last)
        return u // (nh * ns), (u // ns) % nh, u % ns

    return nxt


def _is_first_step():
    return (pl.program_id(0) == 0) & (pl.program_id(1) == 0) & (pl.program_id(2) == 0)


def _lane_bcast(col, rows):
    return jnp.broadcast_to(col, (rows, LANES))


def _attn_pipe_kernel(diff, lam_init, *refs):
    n_in = 10 if diff else 8
    q_ref, qn_ref, kc_ref, kx_ref, kcn_ref, kxn_ref, vc_ref, vx_ref = refs[:8]
    o_ref = refs[n_in]
    s_refs = refs[n_in + 1:n_in + 1 + PIPE_TILES]
    m_refs = refs[n_in + 1 + PIPE_TILES:]
    rows = s_refs[0].shape[0]
    tq = rows // 2 if diff else rows
    nc = kc_ref.shape[1]
    nk = s_refs[0].shape[1]
    lane = lax.broadcasted_iota(jnp.int32, (1, LANES), 1)

    def scores(q, kc, kx, s_ref, m_ref):
        if diff:
            zero = jnp.zeros_like(q)
            q = jnp.concatenate([jnp.where(lane < DA_DK, q, zero), jnp.where(lane < DA_DK, zero, q)], axis=0)
        sc = _dot_nt(q, kc[0])
        sx = _dot_nt(q, kx[0])
        s_ref[:, :nc] = sc
        s_ref[:, nc:] = sx
        m = jnp.maximum(jnp.max(sc, axis=-1, keepdims=True), jnp.max(sx, axis=-1, keepdims=True))
        m_ref[...] = _lane_bcast(m, rows)

    def attend(s_ref, m_ref, out_rows):
        m = jnp.concatenate([m_ref[...]] * 2, axis=1)
        acc = None
        for j in range(nk // MXU_DIM):
            k0 = j * MXU_DIM
            p = jnp.exp2(s_ref[:, k0:k0 + MXU_DIM] - m).astype(BF16)
            v = vc_ref[0, k0:k0 + MXU_DIM, :] if k0 < nc else vx_ref[0, k0 - nc:k0 - nc + MXU_DIM, :]
            part = _dot(p, v)
            acc = part if acc is None else acc + part
        o = acc[:, :LANES] * (1.0 / acc[:, LANES:])
        if diff:
            lqk_ref, og_ref = refs[8:10]
            lqk = lqk_ref[...]
            lam = (jnp.exp(jnp.sum(lqk[0:1] * lqk[1:2], axis=-1, keepdims=True))
                   - jnp.exp(jnp.sum(lqk[2:3] * lqk[3:4], axis=-1, keepdims=True)) + lam_init)
            o = o[:tq] - lam * o[tq:]
            o = o * lax.rsqrt(jnp.mean(o * o, axis=-1, keepdims=True) + EPS) * (og_ref[...] * (1.0 - lam_init))
        o_ref[0, out_rows, :] = o.astype(BF16)

    def tile_rows(t):
        return slice(t * tq, (t + 1) * tq)

    @pl.when(_is_first_step())
    def _():
        for t in range(2):
            scores(q_ref[0, tile_rows(t), :], kc_ref, kx_ref, s_refs[t], m_refs[t])

    n_tiles = q_ref.shape[1] // tq
    for t in range(n_tiles):
        attend(s_refs[t % PIPE_TILES], m_refs[t % PIPE_TILES], tile_rows(t))
        ahead = t + 2
        slot = ahead % PIPE_TILES
        if ahead < n_tiles:
            scores(q_ref[0, tile_rows(ahead), :], kc_ref, kx_ref, s_refs[slot], m_refs[slot])
        else:
            scores(qn_ref[0, tile_rows(ahead - n_tiles), :], kcn_ref, kxn_ref, s_refs[slot], m_refs[slot])


def _attn_pipe(q, kc, kx, vc, vx, tq, diff_params=None):
    diff = diff_params is not None
    bn, t, _ = q.shape
    nh = DA_HEADS if diff else MLA_HEADS
    hw = LANES if diff else 2 * LANES
    step_rows = min(ATTN_STEP_TILES * tq, t)
    ns = t // step_rows
    assert t % step_rows == 0 and (step_rows // tq) % PIPE_TILES == 0
    rows = 2 * tq if diff else tq
    nkeys = kc.shape[1] + kx.shape[1]
    nxt = _next_step_maps(bn, nh, ns)

    def q_next(b, h, i):
        b2, h2, i2 = nxt(b, h, i)
        return b2, i2, h2

    def k_next(b, h, i):
        b2, h2, _ = nxt(b, h, i)
        return b2, 0, h2

    cur = lambda b, h, i: (b, 0, h)
    cspec = lambda a: pl.BlockSpec(a.shape, lambda b, h, i: (0, 0))
    extra = list(diff_params[:2]) if diff else []
    lam_init = diff_params[2] if diff else 0.0
    return pl.pallas_call(
        functools.partial(_attn_pipe_kernel, diff, lam_init),
        grid=(bn, nh, ns),
        in_specs=[pl.BlockSpec((1, step_rows, hw), lambda b, h, i: (b, i, h)),
                  pl.BlockSpec((1, step_rows, hw), q_next),
                  pl.BlockSpec((1, kc.shape[1], hw), cur),
                  pl.BlockSpec((1, kx.shape[1], hw), cur),
                  pl.BlockSpec((1, kc.shape[1], hw), k_next),
                  pl.BlockSpec((1, kx.shape[1], hw), k_next),
                  pl.BlockSpec((1, vc.shape[1], 2 * LANES), cur),
                  pl.BlockSpec((1, vx.shape[1], 2 * LANES), cur)]
                 + [cspec(a) for a in extra],
        out_specs=pl.BlockSpec((1, step_rows, LANES), lambda b, h, i: (b, i, h)),
        out_shape=jax.ShapeDtypeStruct((bn, t, nh * LANES), BF16),
        scratch_shapes=[pltpu.VMEM((rows, nkeys), F32)] * PIPE_TILES + [pltpu.VMEM((rows, LANES), F32)] * PIPE_TILES,
        compiler_params=pltpu.CompilerParams(
            dimension_semantics=("arbitrary",) * 3, vmem_limit_bytes=VMEM_LIMIT),
        name="diff_pipe" if diff else "mla_pipe",
    )(q, q, kc, kx, kc, kx, vc, vx, *extra)


def _post_kernel(ny, *refs):
    x_ref = refs[0]
    y_refs = refs[1:1 + ny]
    ga_ref, shf_ref, scf_ref, gf_ref, g_ref, wo_ref = refs[1 + ny:7 + ny]
    win_ref, wout_ref, o_ref, x1_a, x1_b, h_a, h_b = refs[7 + ny:]
    x1_bufs, h_bufs = (x1_a, x1_b), (h_a, h_b)
    nsub = x_ref.shape[1] // POST_ROWS
    w_rows = np.cumsum([0] + [y.shape[2] for y in y_refs])

    def mix(t):
        rs = slice(t * POST_ROWS, (t + 1) * POST_ROWS)
        attn = functools.reduce(jnp.add, [_dot(y[0, rs, :], wo_ref[int(w_rows[j]):int(w_rows[j + 1]), :])
                                          for j, y in enumerate(y_refs)])
        x1 = x_ref[0, rs, :] + ga_ref[0, 0] * attn
        x1_bufs[t % 2][...] = x1
        h_bufs[t % 2][...] = _modulated_norm(x1, g_ref[...], shf_ref[0, 0], scf_ref[0, 0]).astype(BF16)

    def ffn(t):
        rs = slice(t * POST_ROWS, (t + 1) * POST_ROWS)
        u = _dot(h_bufs[t % 2][...], win_ref[...])
        gate = u[:, :D_FF]
        act = (gate * jax.nn.sigmoid(gate) * u[:, D_FF:]).astype(BF16)
        o_ref[0, rs, :] = x1_bufs[t % 2][...] + gf_ref[0, 0] * _dot(act, wout_ref[...])

    mix(0)
    for t in range(nsub):
        if t + 1 < nsub:
            mix(t + 1)
        ffn(t)


def _post(x, ys, mods, gate_a, shift_f, scale_f, gate_f, gain, layer, w_o, w_in, w_out, tm):
    bn, t, d = x.shape
    tok = lambda w: pl.BlockSpec((1, tm, w), lambda b, i: (b, i, 0))
    wspec = lambda a: pl.BlockSpec((None,) + a.shape[1:], lambda b, i: (layer, 0, 0), pipeline_mode=pl.Buffered(1))
    return pl.pallas_call(
        functools.partial(_post_kernel, len(ys)),
        grid=(bn, t // tm),
        in_specs=[tok(d)] + [tok(y.shape[2]) for y in ys]
                 + [_mod_spec(gate_a), _mod_spec(shift_f), _mod_spec(scale_f), _mod_spec(gate_f), _const_spec(gain)]
                 + [wspec(w_o), wspec(w_in), wspec(w_out)],
        out_specs=tok(d),
        out_shape=jax.ShapeDtypeStruct((bn, t, d), F32),
        scratch_shapes=[pltpu.VMEM((POST_ROWS, d), F32)] * 2 + [pltpu.VMEM((POST_ROWS, d), BF16)] * 2,
        compiler_params=pltpu.CompilerParams(
            dimension_semantics=("arbitrary", "arbitrary"), vmem_limit_bytes=VMEM_LIMIT),
        name="post",
    )(x, *ys, mods, mods, mods, mods, gain, w_o, w_in, w_out)


def _proj_odd_kernel(*refs):
    for t in range(refs[0].shape[1] // PROJ_ROWS):
        _proj_odd_tile(slice(t * PROJ_ROWS, (t + 1) * PROJ_ROWS), *refs)


def _proj_odd_tile(rs, x_ref, sh_ref, sc_ref, g_ref, w_ref, g64_ref, qg_ref, kg_ref, q_ref, k_ref, v_ref):
    h = _modulated_norm(x_ref[0, rs, :], g_ref[...], sh_ref[0, 0], sc_ref[0, 0]).astype(BF16)
    p = _dot(h, w_ref[...])
    w = NA_HEADS * NA_DH

    def qk(v, gain, scale):
        return (v * _group_rsqrt(v, g64_ref[...], NA_DH) * (gain * scale)).astype(BF16)

    q_ref[0, rs, :] = qk(p[:, :w], qg_ref[...], NA_DH ** -0.5 * LOG2E)
    k_ref[0, rs, :] = qk(p[:, w:2 * w], kg_ref[...], 1.0)
    v = p[:, 2 * w:].astype(BF16)
    ones = jnp.ones((p.shape[0], LANES), BF16)
    v_ref[0, rs, :] = jnp.concatenate(
        [blk for j in range(w // LANES) for blk in (v[:, j * LANES:(j + 1) * LANES], ones)], axis=1)


def _proj_odd(x, mods, shift, scale, gain, consts, tm):
    bn, t, d = x.shape
    tok = lambda w: pl.BlockSpec((1, tm, w), lambda b, i: (b, i, 0))
    w = NA_HEADS * NA_DH
    widths = (w, w, 2 * w)
    return pl.pallas_call(
        _proj_odd_kernel,
        grid=(bn, t // tm),
        in_specs=[tok(d), _mod_spec(shift), _mod_spec(scale), _const_spec(gain)] + [_const_spec(c) for c in consts],
        out_specs=[tok(n) for n in widths],
        out_shape=[jax.ShapeDtypeStruct((bn, t, n), BF16) for n in widths],
        compiler_params=pltpu.CompilerParams(
            dimension_semantics=("arbitrary", "arbitrary"), vmem_limit_bytes=VMEM_LIMIT),
        name="proj_odd",
    )(x, mods, mods, gain, *consts)


def _na_window_start(g):
    return min(max(NA_GROUP * g - NA_KH // 2, 0), NA_ROWS - NA_WIN_ROWS)


def _na_bias_tile(g, bias_r, lo):
    ws = _na_window_start(g)
    neg = jnp.full((GRID_W, LANES), NEG_INF, F32)
    row_tiles = []
    for head in range(2):
        for a in range(NA_GROUP):
            r = NA_GROUP * g + a
            rs = min(max(r - NA_KH // 2, 0), NA_ROWS - NA_KH)
            blocks = []
            for jp in range(NA_WIN_ROWS // 2):
                kr0 = ws + 2 * jp
                ok0 = rs <= kr0 < rs + NA_KH
                ok1 = rs <= kr0 + 1 < rs + NA_KH
                if not (ok0 or ok1):
                    blocks.append(neg)
                    continue
                blk = bias_r[0, head, kr0 - r + NA_KH - 1]
                if not ok0:
                    blk = jnp.where(lo, neg, blk)
                if not ok1:
                    blk = jnp.where(lo, blk, neg)
                blocks.append(blk)
            row_tiles.append(jnp.concatenate(blocks, axis=1))
    return jnp.concatenate(row_tiles, axis=0)


def _na_kernel(q_ref, k_ref, v_ref, kc_ref, vc_ref, bias_ref, qn_ref, kn_ref, kcn_ref, biasn_ref, o_ref,
               *scratch):
    s_refs = scratch[:PIPE_TILES]
    m_refs = scratch[PIPE_TILES:]
    lane = lax.broadcasted_iota(jnp.int32, (1, LANES), 1)
    lo = lane < NA_DH
    gq = NA_GROUP * GRID_W
    nloc = NA_WIN_ROWS * GRID_W
    nk = s_refs[0].shape[1]

    def win(g):
        k0 = _na_window_start(g) * GRID_W
        return slice(k0, k0 + nloc)

    def scores(bi, g, q_r, k_r, kc_r, bias_r, s_ref, m_ref):
        q = q_r[bi, g * gq:(g + 1) * gq, :]
        zero = jnp.zeros_like(q)
        qq = jnp.concatenate([jnp.where(lo, q, zero), jnp.where(lo, zero, q)], axis=0)
        s_loc = _dot_nt(qq, k_r[bi, win(g), :]) + _na_bias_tile(g, bias_r, lo)
        s_ctx = _dot_nt(qq, kc_r[bi])
        s_ref[:, :nloc] = s_loc
        s_ref[:, nloc:] = s_ctx
        m = jnp.maximum(jnp.max(s_loc, axis=-1, keepdims=True), jnp.max(s_ctx, axis=-1, keepdims=True))
        m_ref[...] = _lane_bcast(m, 2 * gq)

    def attend(bi, g, s_ref, m_ref):
        m = jnp.concatenate([m_ref[...]] * 2, axis=1)
        k0 = _na_window_start(g) * GRID_W
        acc = None
        for j in range(nk // MXU_DIM):
            c0 = j * MXU_DIM
            p = jnp.exp2(s_ref[:, c0:c0 + MXU_DIM] - m).astype(BF16)
            v = v_ref[bi, k0 + c0:k0 + c0 + MXU_DIM, :] if c0 < nloc else vc_ref[bi, c0 - nloc:c0 - nloc + MXU_DIM, :]
            part = _dot(p, v)
            acc = part if acc is None else acc + part
        o = acc[:, :LANES] * (1.0 / acc[:, LANES:])
        o_ref[bi, g * gq:(g + 1) * gq, :] = jnp.where(lo, o[:gq], o[gq:]).astype(BF16)

    @pl.when((pl.program_id(0) == 0) & (pl.program_id(1) == 0))
    def _():
        for g in range(2):
            scores(0, g, q_ref, k_ref, kc_ref, bias_ref, s_refs[g], m_refs[g])

    n_groups = q_ref.shape[0] * NA_GROUPS
    for gg in range(n_groups):
        attend(gg // NA_GROUPS, gg % NA_GROUPS, s_refs[gg % PIPE_TILES], m_refs[gg % PIPE_TILES])
        ahead = gg + 2
        slot = ahead % PIPE_TILES
        if ahead < n_groups:
            scores(ahead // NA_GROUPS, ahead % NA_GROUPS, q_ref, k_ref, kc_ref, bias_ref, s_refs[slot], m_refs[slot])
        else:
            scores(0, ahead - n_groups, qn_ref, kn_ref, kcn_ref, biasn_ref, s_refs[slot], m_refs[slot])


def _na_attn(q, k, v, kc, vc, bias):
    bn, s, w = q.shape
    assert s == NA_ROWS * GRID_W and NA_GROUPS % PIPE_TILES == 0
    pairs = w // LANES
    nb = bn // NA_STEP_BATCH
    assert bn % NA_STEP_BATCH == 0
    last = pairs * nb - 1

    def nxt(p, b):
        u = jnp.minimum(p * nb + b + 1, last)
        return (u % nb) * NA_STEP_BATCH, 0, u // nb

    cur = lambda p, b: (b, 0, p)
    spec = lambda a, width: pl.BlockSpec((NA_STEP_BATCH, a.shape[1], width), cur)
    spec_next = lambda a, width: pl.BlockSpec((1, a.shape[1], width), nxt)
    rows = 2 * NA_GROUP * GRID_W
    nk = NA_WIN_ROWS * GRID_W + kc.shape[1]
    return pl.pallas_call(
        _na_kernel,
        grid=(pairs, nb),
        in_specs=[spec(q, LANES), spec(k, LANES), spec(v, 2 * LANES),
                  spec(kc, LANES), spec(vc, 2 * LANES),
                  pl.BlockSpec((1,) + bias.shape[1:], lambda p, b: (p, 0, 0, 0, 0)),
                  spec_next(q, LANES), spec_next(k, LANES), spec_next(kc, LANES),
                  pl.BlockSpec((1,) + bias.shape[1:], lambda p, b: (nxt(p, b)[2], 0, 0, 0, 0))],
        out_specs=spec(q, LANES),
        out_shape=jax.ShapeDtypeStruct((bn, s, w), BF16),
        scratch_shapes=[pltpu.VMEM((rows, nk), F32)] * PIPE_TILES + [pltpu.VMEM((rows, LANES), F32)] * PIPE_TILES,
        compiler_params=pltpu.CompilerParams(
            dimension_semantics=("arbitrary", "arbitrary"), vmem_limit_bytes=VMEM_LIMIT),
        name="na_attn",
    )(q, k, v, kc, vc, bias, q, k, kc, bias)


def _group_matrix(width, group):
    idx = np.arange(width) // group
    return jnp.asarray((idx[:, None] == idx[None, :]).astype(np.float32), dtype=BF16)


def _rope_tables(s):
    t = np.arange(s)
    row = (t // GRID_W).astype(np.float32)
    col = (t % GRID_W).astype(np.float32)
    half = DA_DK // 2
    inv = (np.float32(ROPE_THETA) ** (-np.arange(0, half, 2, dtype=np.float32) / np.float32(half))).astype(np.float32)
    ar = row[:, None] * inv
    ac = col[:, None] * inv
    cr, sr, cc, sc = np.cos(ar), np.sin(ar), np.cos(ac), np.sin(ac)
    cos64 = np.concatenate([cr, cr, cc, cc], axis=1)
    sin64 = np.concatenate([-sr, sr, -sc, sc], axis=1)
    return (jnp.asarray(np.concatenate([cos64, cos64], axis=1), F32),
            jnp.asarray(np.concatenate([sin64, sin64], axis=1), F32))


def _na_bias_table(rpb):
    cols = np.arange(GRID_W)
    col_start = np.clip(cols - NA_KW // 2, 0, GRID_W - NA_KW)
    col_ok = (cols[None, :] >= col_start[:, None]) & (cols[None, :] < col_start[:, None] + NA_KW)
    dc_idx = np.clip(cols[None, :] - cols[:, None], -(NA_KW - 1), NA_KW - 1) + NA_KW - 1
    nrel = 2 * NA_KW - 1
    rpb_ext = jnp.concatenate([rpb * LOG2E, jnp.full(rpb.shape[:2] + (1,), NEG_INF, F32)], axis=-1)
    by_col = rpb_ext[:, :, np.where(col_ok, dc_idx, nrel)]
    pairs = jnp.concatenate([by_col[:, :-1], by_col[:, 1:]], axis=-1)
    return pairs.reshape(NA_HEADS // 2, 2, 2 * NA_KH - 2, GRID_W, 2 * GRID_W)


def _tile_lanes(g, n):
    return jnp.tile(g, n).reshape(1, -1)


def kernel(x, c, ctx, c_ctx, mod_w, mod_b, norm_mix_g, norm_ffn_g, w_out, ffn_w_in, ffn_w_out, ev_w_in, da_q_g, da_k_g, da_lq1, da_lk1, da_lq2, da_lk2, da_out_g, mla_q_a_g, mla_w_uq, mla_kv_a_g, mla_w_ukv, mla_q_g, mla_k_g, mla_kr_g, od_w_in, na_q_g, na_k_g, na_rpb):
    bsz, seq, d = x.shape
    tm = 256
    tm_x = 4 * PROJ_ROWS
    tq = 256

    cond = jnp.concatenate([c, c_ctx[None], jnp.zeros((7, d), F32)], axis=0)
    mods = _adaln(cond, mod_w, mod_b)

    def mod_vectors(l, row):
        return [(l, row, j) for j in range(6)]

    g64 = _group_matrix(MXU_DIM, 64)
    cos, sin = _rope_tables(seq)

    l = 0
    w_in = jnp.concatenate([ev_w_in[0], jnp.zeros((d, 64), F32)], axis=1).astype(BF16)
    wuq = mla_w_uq[0].reshape(MLA_Q_LORA, MLA_HEADS, MLA_NOPE + MLA_ROPE)
    wuq = jnp.pad(wuq, ((0, 0), (0, 0), (0, 64))).reshape(MLA_Q_LORA, MLA_HEADS * 256).astype(BF16)
    mqg = jnp.tile(jnp.pad(mla_q_g[0], (0, 64)), MLA_HEADS).reshape(1, -1)
    consts = [
        w_in, g64, _group_matrix(LANES, LANES), _group_matrix(MXU_DIM, MXU_DIM),
        _tile_lanes(da_q_g[0], 8), _tile_lanes(da_k_g[0], 8),
        mla_q_a_g[0].reshape(1, -1), wuq, mla_kv_a_g[0].reshape(1, -1), mla_w_ukv[0].astype(BF16),
        mqg, mla_k_g[0].reshape(1, -1), jnp.pad(mla_kr_g[0], (0, 64)).reshape(1, -1),
    ]
    lqk = jnp.stack([da_lq1[0], da_lk1[0], da_lq2[0], da_lk2[0]])
    lam_init = 0.8 - 0.6 * math.exp(-0.3 * l)
    out_g = da_out_g[0].reshape(1, -1)
    w_o = w_out.astype(BF16)
    f_in = ffn_w_in.astype(BF16)
    f_out = ffn_w_out.astype(BF16)
    gm = norm_mix_g[l].reshape(1, d)
    gf = norm_ffn_g[l].reshape(1, d)

    sh_a, sc_a, g_a, sh_f, sc_f, g_f = mod_vectors(l, None)
    csh_a, csc_a, cg_a, csh_f, csc_f, cg_f = mod_vectors(l, bsz)

    px = _proj_even(x, mods, sh_a, sc_a, gm, consts, cos, sin, True, tm_x)
    pc = _proj_even(ctx, mods, csh_a, csc_a, gm, consts, cos[:ctx.shape[1]], sin[:ctx.shape[1]], False, tm)
    dq_x, dk_x, dv_x, mq_x, mk_x, mv_x = px
    dq_c, dk_c, dv_c, mq_c, mk_c, mv_c = pc

    da_x = _attn_pipe(dq_x, dk_c, dk_x, dv_c, dv_x, tq, (lqk, out_g, lam_init))
    mla_x = _attn_pipe(mq_x, mk_c, mk_x, mv_c, mv_x, tq)
    da_c = _ctx_diff_attn(dq_c, dk_c, dv_c, lqk, out_g, lam_init)
    mla_c = _ctx_mla_attn(mq_c, mk_c, mv_c)

    x = _post(x, [da_x, mla_x], mods, g_a, sh_f, sc_f, g_f, gf, l, w_o, f_in, f_out, POST_STEP_TILES * POST_ROWS)
    ctx = _post(ctx, [da_c, mla_c], mods, cg_a, csh_f, csc_f, cg_f, gf, l, w_o, f_in, f_out, tm)

    l = 1
    sh_a, sc_a, g_a, sh_f, sc_f, g_f = mod_vectors(l, None)
    csh_a, csc_a = mod_vectors(l, bsz)[:2]
    gm = norm_mix_g[l].reshape(1, d)
    gf = norm_ffn_g[l].reshape(1, d)
    consts = [od_w_in[0].astype(BF16), g64, _tile_lanes(na_q_g[0], NA_HEADS), _tile_lanes(na_k_g[0], NA_HEADS)]
    q_x, k_x, v_x = _proj_odd(x, mods, sh_a, sc_a, gm, consts, tm_x)
    _, k_c, v_c = _proj_odd(ctx, mods, csh_a, csc_a, gm, consts, tm)
    y = _na_attn(q_x, k_x, v_x, k_c, v_c, _na_bias_table(na_rpb[0]))
    x = _post(x, [y], mods, g_a, sh_f, sc_f, g_f, gf, l, w_o, f_in, f_out, POST_STEP_TILES * POST_ROWS)
    return x
```
